```python
import jax, jax.numpy as jnp
from jax import lax
import numpy as np

D_MODEL = 1024
BATCH = 32
SEQ = 2048
DEPTH = 4
DEC_BATCH = 32
DEC_SEQ = 32
PAST_LEN = 1024

CHUNK = 64
EPS = 1e-6
NEG = -1e30
N_HEADS_A = 16
N_KV_A = 4
GROUP_A = N_HEADS_A // N_KV_A
HEAD_DIM_A = D_MODEL // N_HEADS_A
WINDOW = 128
WIN_CHUNKS = WINDOW // CHUNK
BAND = (WIN_CHUNKS + 1) * CHUNK
N_HEADS_B = 4
DV_B = D_MODEL // N_HEADS_B
DK_B = DV_B // 2
GATE_RANK = 16
GATE_TAU = 16.0
GLA_BLOCK = 16
D_FF = (((8 * D_MODEL + 2) // 3 + 255) // 256) * 256

WA_Q = N_HEADS_A * HEAD_DIM_A
WA_KV = N_KV_A * HEAD_DIM_A
WB_K = N_HEADS_B * DK_B
WB_V = N_HEADS_B * DV_B
SPLIT_SIZES = (WA_Q, WA_KV, WA_KV, WB_K, WB_K, WB_V, WB_V, GATE_RANK, D_MODEL, D_MODEL)
SPLIT_POINTS = tuple(int(v) for v in np.cumsum(SPLIT_SIZES)[:-1])
P_IN = int(sum(SPLIT_SIZES))

kernel_name = "hybrid_swa_sink_gla_stream_step"


def _rmsnorm(x, g):
    xf = x.astype(jnp.float32)
    y = xf * lax.rsqrt(jnp.mean(xf * xf, axis=-1, keepdims=True) + EPS)
    return (y * g.astype(jnp.float32)).astype(x.dtype)


def _alibi_slopes():
    return 2.0 ** (-8.0 * jnp.arange(1, N_HEADS_A + 1, dtype=jnp.float32) / N_HEADS_A)


def _alibi_band_bias(qpos, kpos):
    qc = qpos // CHUNK
    kc = kpos // CHUNK
    ok = ((kpos[:, None, :] >= 0) & (kc[:, None, :] <= qc[:, :, None])
          & (kc[:, None, :] >= qc[:, :, None] - WIN_CHUNKS))
    dist = jnp.abs(qpos[:, :, None] - kpos[:, None, :]).astype(jnp.float32)
    slopes = _alibi_slopes().reshape(N_KV_A, GROUP_A)
    bias = -slopes[None, :, :, None, None] * dist[:, None, None]
    return jnp.where(ok[:, None, None], bias, NEG)


def _band(x):
    B, T = x.shape[:2]
    nC = T // CHUNK
    xp = jnp.pad(x, ((0, 0), (WINDOW, 0), (0, 0), (0, 0)))
    xc = xp.reshape(B, nC + WIN_CHUNKS, CHUNK, *x.shape[2:])
    return jnp.concatenate([xc[:, j:j + nC] for j in range(WIN_CHUNKS + 1)], axis=2)


def _sink_attend(q, k, v, bias, sinks):
    s = jnp.einsum('bnqhgd,bnkhd->bnhgqk', q, k).astype(jnp.float32) * (HEAD_DIM_A ** -0.5) + bias
    sink = sinks.astype(jnp.float32).reshape(N_KV_A, GROUP_A)[None, None, :, :, None, None]
    m = jnp.maximum(jnp.max(s, axis=-1, keepdims=True), sink)
    p = jnp.exp(s - m)
    p = p / (jnp.sum(p, axis=-1, keepdims=True) + jnp.exp(sink - m))
    return jnp.einsum('bnhgqk,bnkhd->bnqhgd', p.astype(v.dtype), v)


def _gla(q, k, v, log_a, s0):
    B, T = q.shape[:2]
    nb = -(-T // GLA_BLOCK)
    pad = nb * GLA_BLOCK - T
    padf = lambda a: jnp.pad(a, ((0, 0), (0, pad), (0, 0), (0, 0)))
    blk = lambda a: padf(a).reshape(B, nb, GLA_BLOCK, *a.shape[2:]).astype(jnp.float32)
    qb, kb, vb, gb = blk(q), blk(k), blk(v), blk(log_a)
    cum = jnp.cumsum(gb, axis=2)
    last = cum[:, :, -1:]
    q_in = qb * jnp.exp(cum)
    k_in = kb * jnp.exp(-cum)
    k_out = kb * jnp.exp(last - cum)
    causal = jnp.tril(jnp.ones((GLA_BLOCK, GLA_BLOCK), jnp.float32))
    att = jnp.einsum('bnthd,bnshd->bnhts', q_in, k_in) * causal
    o_intra = jnp.einsum('bnhts,bnshv->bnthv', att, vb)
    dec = jnp.exp(last[:, :, 0])

    def step(S, xs):
        q_i, k_o, v_i, d_i = xs
        o = jnp.einsum('bthd,bhdv->bthv', q_i, S)
        S = S * d_i[..., None] + jnp.einsum('bshd,bshv->bhdv', k_o, v_i)
        return S, o

    xs = (jnp.moveaxis(q_in, 1, 0), jnp.moveaxis(k_out, 1, 0), jnp.moveaxis(vb, 1, 0), jnp.moveaxis(dec, 1, 0))
    S_T, o_inter = lax.scan(step, s0.astype(jnp.float32), xs)
    o = o_intra + jnp.moveaxis(o_inter, 0, 1)
    o = o.reshape(B, nb * GLA_BLOCK, *o.shape[3:])[:, :T]
    return o.astype(v.dtype), S_T


def _mix_projections(h, w_in, w_a2, b_a, g_q, g_k):
    B, T, _ = h.shape
    qa, ka, va, qb, kb, vb, rb, a1, ga, gb = jnp.split(h @ w_in, SPLIT_POINTS, axis=-1)
    qa = _rmsnorm(qa.reshape(B, T, N_HEADS_A, HEAD_DIM_A), g_q)
    ka = _rmsnorm(ka.reshape(B, T, N_KV_A, HEAD_DIM_A), g_k)
    va = va.reshape(B, T, N_KV_A, HEAD_DIM_A)
    qb = qb.reshape(B, T, N_HEADS_B, DK_B) * (DK_B ** -0.5)
    kb = kb.reshape(B, T, N_HEADS_B, DK_B)
    vb = vb.reshape(B, T, N_HEADS_B, DV_B)
    z = (a1 @ w_a2 + b_a).astype(jnp.float32)
    log_a = (jax.nn.log_sigmoid(z) / GATE_TAU).reshape(B, T, N_HEADS_B, DK_B)
    return qa, ka, va, qb, kb, vb, log_a, rb, ga, gb


def _merge(x, oa, ob, rb, ga, gb, g_ob, w_out):
    B, T, _ = x.shape
    ob = _rmsnorm(ob, g_ob).reshape(B, T, D_MODEL) * jax.nn.silu(rb)
    m = jax.nn.sigmoid(ga) * oa + jax.nn.sigmoid(gb) * ob
    return x + m @ w_out


def _ffn(x, g, w_gate, w_up, w_down):
    h = _rmsnorm(x, g)
    return x + (jax.nn.silu(h @ w_gate) * (h @ w_up)) @ w_down


def setup_inputs(seed: int = 0) -> dict:
    key = jax.random.key(seed)
    ks = jax.random.split(key, 20)
    nrm = lambda k, shape, s: jax.random.normal(k, shape, jnp.float32) * s
    L = min(WINDOW, PAST_LEN)
    return {
        "x_prompt": nrm(ks[0], (BATCH, SEQ, D_MODEL), 1.0),
        "x_sample": nrm(ks[1], (DEC_BATCH, DEC_SEQ, D_MODEL), 1.0),
        "cache_swa_k": nrm(ks[2], (DEPTH, DEC_BATCH, L, N_KV_A, HEAD_DIM_A), 1.0),
        "cache_swa_v": nrm(ks[3], (DEPTH, DEC_BATCH, L, N_KV_A, HEAD_DIM_A), 1.0),
        "state_gla": nrm(ks[4], (DEPTH, DEC_BATCH, N_HEADS_B, DK_B, DV_B), 0.5),
        "w_in": nrm(ks[5], (DEPTH, D_MODEL, P_IN), D_MODEL ** -0.5),
        "w_a2": nrm(ks[6], (DEPTH, GATE_RANK, WB_K), GATE_RANK ** -0.5),
        "b_a": nrm(ks[7], (DEPTH, WB_K), 0.1),
        "g_q": 1.0 + nrm(ks[8], (DEPTH, HEAD_DIM_A), 0.02),
        "g_k": 1.0 + nrm(ks[9], (DEPTH, HEAD_DIM_A), 0.02),
        "sinks": nrm(ks[10], (DEPTH, N_HEADS_A), 0.5),
        "g_ob": 1.0 + nrm(ks[11], (DEPTH, DV_B), 0.02),
        "w_out": nrm(ks[12], (DEPTH, D_MODEL, D_MODEL), D_MODEL ** -0.5),
        "g_mix": 1.0 + nrm(ks[13], (DEPTH, D_MODEL), 0.02),
        "g_ffn": 1.0 + nrm(ks[14], (DEPTH, D_MODEL), 0.02),
        "w_gate": nrm(ks[15], (DEPTH, D_MODEL, D_FF), D_MODEL ** -0.5),
        "w_up": nrm(ks[16], (DEPTH, D_MODEL, D_FF), D_MODEL ** -0.5),
        "w_down": nrm(ks[17], (DEPTH, D_FF, D_MODEL), D_FF ** -0.5),
    }


def reference(x_prompt, x_sample, cache_swa_k, cache_swa_v, state_gla, w_in, w_a2, b_a,
              g_q, g_k, sinks, g_ob, w_out, g_mix, g_ffn, w_gate, w_up, w_down):
    B, T, _ = x_prompt.shape
    nC = T // CHUNK
    qpos_p = jnp.arange(T, dtype=jnp.int32).reshape(nC, CHUNK)
    kpos_p = (jnp.arange(nC, dtype=jnp.int32)[:, None] * CHUNK - WINDOW
              + jnp.arange(BAND, dtype=jnp.int32)[None, :])
    bias_p = _alibi_band_bias(qpos_p, kpos_p)
    x = x_prompt
    pk, pv, ps = [], [], []
    for l in range(DEPTH):
        h = _rmsnorm(x, g_mix[l])
        qa, ka, va, qb, kb, vb, log_a, rb, ga, gb = _mix_projections(h, w_in[l], w_a2[l], b_a[l], g_q[l], g_k[l])
        oa = _sink_attend(qa.reshape(B, nC, CHUNK, N_KV_A, GROUP_A, HEAD_DIM_A),
                          _band(ka), _band(va), bias_p, sinks[l]).reshape(B, T, D_MODEL)
        ob, s_new = _gla(qb, kb, vb, log_a, jnp.zeros((B, N_HEADS_B, DK_B, DV_B), jnp.float32))
        x = _merge(x, oa, ob, rb, ga, gb, g_ob[l], w_out[l])
        x = _ffn(x, g_ffn[l], w_gate[l], w_up[l], w_down[l])
        pk.append(ka[:, T - WINDOW:])
        pv.append(va[:, T - WINDOW:])
        ps.append(s_new)
    y_prompt = x

    Bd, Tq, _ = x_sample.shape
    L = cache_swa_k.shape[2]
    qpos_s = (PAST_LEN + jnp.arange(Tq, dtype=jnp.int32))[None]
    kpos_s = (PAST_LEN - L + jnp.arange(L + Tq, dtype=jnp.int32))[None]
    bias_s = _alibi_band_bias(qpos_s, kpos_s)
    x = x_sample
    sk, sv, ss = [], [], []
    for l in range(DEPTH):
        h = _rmsnorm(x, g_mix[l])
        qa, ka, va, qb, kb, vb, log_a, rb, ga, gb = _mix_projections(h, w_in[l], w_a2[l], b_a[l], g_q[l], g_k[l])
        k_all = jnp.concatenate([cache_swa_k[l].astype(ka.dtype), ka], axis=1)[:, None]
        v_all = jnp.concatenate([cache_swa_v[l].astype(va.dtype), va], axis=1)[:, None]
        oa = _sink_attend(qa.reshape(Bd, 1, Tq, N_KV_A, GROUP_A, HEAD_DIM_A),
                          k_all, v_all, bias_s, sinks[l]).reshape(Bd, Tq, D_MODEL)
        ob, s_new = _gla(qb, kb, vb, log_a, state_gla[l])
        x = _merge(x, oa, ob, rb, ga, gb, g_ob[l], w_out[l])
        x = _ffn(x, g_ffn[l], w_gate[l], w_up[l], w_down[l])
        sk.append(ka)
        sv.append(va)
        ss.append(s_new)
    y_sample = x

    return (y_prompt, y_sample, jnp.stack(pk), jnp.stack(pv), jnp.stack(ps),
            jnp.stack(sk), jnp.stack(sv), jnp.stack(ss))
```

```python
import functools

import jax
import jax.numpy as jnp
import numpy as np
from jax import lax
from jax.experimental import pallas as pl
from jax.experimental.pallas import tpu as pltpu

CHUNK = 64
EPS = 1e-6
NEG = -1e30
N_HEADS_A = 16
N_KV_A = 4
GROUP_A = N_HEADS_A // N_KV_A
HEAD_DIM_A = 64
WINDOW = 128
N_HEADS_B = 4
DK_B = 128
DV_B = 256
GATE_RANK = 16
GATE_TAU = 16.0
GLA_BLOCK = 16

LANES = 128
A1_PAD = LANES
VMEM_LIMIT = 56 * 1024 * 1024

BF16 = jnp.bfloat16
F32 = jnp.float32


def _cparams(*sem):
    return pltpu.CompilerParams(dimension_semantics=sem, vmem_limit_bytes=VMEM_LIMIT)


def _resident(shape):
    nd = len(shape)
    return pl.BlockSpec(shape, lambda *_: (0,) * nd, pipeline_mode=pl.Buffered(1))


def _rms_rows(x, g):
    ms = jnp.mean(x * x, axis=-1, keepdims=True)
    return (x * lax.rsqrt(ms + EPS)) * g


def _inproj_body(x_ref, g_ref, wa_ref, wb_ref, wc_ref, oa_ref, ob_ref, oc_ref, *, col_chunk):
    h = _rms_rows(x_ref[...], g_ref[...]).astype(BF16)
    for w_ref, o_ref in ((wa_ref, oa_ref), (wb_ref, ob_ref), (wc_ref, oc_ref)):
        n = w_ref.shape[1]
        for c0 in range(0, n, col_chunk):
            c1 = min(c0 + col_chunk, n)
            o_ref[:, c0:c1] = jnp.dot(h, w_ref[:, c0:c1], preferred_element_type=F32).astype(BF16)


def _inproj(x, g, wa, wb, wc, *, tm):
    m, d = x.shape
    outs = [jax.ShapeDtypeStruct((m, w.shape[1]), BF16) for w in (wa, wb, wc)]
    return pl.pallas_call(
        functools.partial(_inproj_body, col_chunk=512),
        grid=(m // tm,),
        in_specs=[pl.BlockSpec((tm, d), lambda i: (i, 0)), _resident(g.shape),
                  _resident(wa.shape), _resident(wb.shape), _resident(wc.shape)],
        out_specs=[pl.BlockSpec((tm, w.shape[1]), lambda i: (i, 0)) for w in (wa, wb, wc)],
        out_shape=outs,
        compiler_params=_cparams("parallel"),
        name="inproj",
    )(x, g, wa, wb, wc)


def _head_sumsq(x, bd):
    sq = x * x
    hi = sq.astype(BF16)
    lo = (sq - hi.astype(F32)).astype(BF16)
    return (jnp.dot(hi, bd, preferred_element_type=F32) + jnp.dot(lo, bd, preferred_element_type=F32))


def _block_diag_ones():
    r = lax.broadcasted_iota(jnp.int32, (LANES, LANES), 0) // HEAD_DIM_A
    c = lax.broadcasted_iota(jnp.int32, (LANES, LANES), 1) // HEAD_DIM_A
    return jnp.where(r == c, 1.0, 0.0).astype(BF16)


def _head_norm(x, g2, bd):
    ss = _head_sumsq(x, bd)
    return (x * lax.rsqrt(ss * (1.0 / HEAD_DIM_A) + EPS)) * g2


def _dup_halves(x, lo_half):
    r = pltpu.roll(x, HEAD_DIM_A, axis=1)
    return jnp.where(lo_half, x, r), jnp.where(lo_half, r, x)


def _attend_group(lhs, kk, vv, bias, sink_col, lo_half, cq):
    s = lax.dot_general(lhs, kk, (((1,), (1,)), ((), ())), preferred_element_type=F32) + bias
    m = jnp.maximum(jnp.max(s, axis=-1, keepdims=True), sink_col)
    p = jnp.exp(s - m)
    l = jnp.sum(p, axis=-1, keepdims=True) + jnp.exp(sink_col - m)
    o = jnp.dot(p.astype(BF16), vv, preferred_element_type=F32) / l
    out_a = jnp.where(lo_half, o[0:cq], o[cq:2 * cq])
    out_b = jnp.where(lo_half, o[2 * cq:3 * cq], o[3 * cq:4 * cq])
    return out_a.astype(BF16), out_b.astype(BF16)


def _sink_col(sink_ref, j, cq):
    return jnp.concatenate([jnp.full((cq, 1), sink_ref[GROUP_A * j + g], F32) for g in range(GROUP_A)], axis=0)


def _swa_prompt_body(sink_ref, q_ref, kv_ref, bias_ref, gq_ref, gk_ref,
                     o_ref, klast_ref, vlast_ref, kk_s, vv_s, qm_s, *, tq):
    i = pl.program_id(1)
    kvw = N_KV_A * HEAD_DIM_A
    lane = lax.broadcasted_iota(jnp.int32, (1, LANES), 1)
    lo_half = lane < HEAD_DIM_A
    bd = _block_diag_ones()

    @pl.when(i == 0)
    def _():
        kk_s[:, 0:WINDOW, :] = jnp.zeros((N_KV_A, WINDOW, LANES), BF16)
        vv_s[:, 0:WINDOW, :] = jnp.zeros((N_KV_A, WINDOW, LANES), BF16)

    @pl.when(i > 0)
    def _():
        kk_s[:, 0:WINDOW, :] = kk_s[:, tq:tq + WINDOW, :]
        vv_s[:, 0:WINDOW, :] = vv_s[:, tq:tq + WINDOW, :]

    for t in range(kvw // LANES):
        k_t = _head_norm(kv_ref[:, t * LANES:(t + 1) * LANES].astype(F32), gk_ref[...], bd)
        v_t = kv_ref[:, kvw + t * LANES:kvw + (t + 1) * LANES].astype(F32)
        klast_ref[0, :, t * LANES:(t + 1) * LANES] = k_t[tq - WINDOW:, :]
        vlast_ref[0, :, t * LANES:(t + 1) * LANES] = v_t[tq - WINDOW:, :]
        k0, k1 = _dup_halves(k_t, lo_half)
        v0, v1 = _dup_halves(v_t, lo_half)
        kk_s[2 * t, WINDOW:, :] = k0.astype(BF16)
        kk_s[2 * t + 1, WINDOW:, :] = k1.astype(BF16)
        vv_s[2 * t, WINDOW:, :] = v0.astype(BF16)
        vv_s[2 * t + 1, WINDOW:, :] = v1.astype(BF16)

    scale = HEAD_DIM_A ** -0.5
    for t in range(N_HEADS_A * HEAD_DIM_A // LANES):
        q_t = _head_norm(q_ref[:, t * LANES:(t + 1) * LANES].astype(F32), gq_ref[...], bd) * scale
        qm_s[0, :, t * LANES:(t + 1) * LANES] = jnp.where(lo_half, q_t, 0.0).astype(BF16)
        qm_s[1, :, t * LANES:(t + 1) * LANES] = jnp.where(lo_half, 0.0, q_t).astype(BF16)

    band = WINDOW + CHUNK
    key_idx = lax.broadcasted_iota(jnp.int32, (1, band), 1)

    def chunk(c, carry):
        r0 = pl.multiple_of(c * CHUNK, CHUNK)
        kpos = key_idx + (i * tq + c * CHUNK - WINDOW)
        mask_row = jnp.where(kpos >= 0, 0.0, NEG)
        for j in range(N_KV_A):
            ta, tb = 2 * j, 2 * j + 1
            lhs = jnp.concatenate([
                qm_s[0, pl.ds(r0, CHUNK), ta * LANES:(ta + 1) * LANES],
                qm_s[1, pl.ds(r0, CHUNK), ta * LANES:(ta + 1) * LANES],
                qm_s[0, pl.ds(r0, CHUNK), tb * LANES:(tb + 1) * LANES],
                qm_s[1, pl.ds(r0, CHUNK), tb * LANES:(tb + 1) * LANES]], axis=0)
            out_a, out_b = _attend_group(
                lhs, kk_s[j, pl.ds(r0, band), :], vv_s[j, pl.ds(r0, band), :],
                bias_ref[j] + mask_row, _sink_col(sink_ref, j, CHUNK), lo_half, CHUNK)
            o_ref[pl.ds(r0, CHUNK), ta * LANES:(ta + 1) * LANES] = out_a
            o_ref[pl.ds(r0, CHUNK), tb * LANES:(tb + 1) * LANES] = out_b
        return carry

    lax.fori_loop(0, tq // CHUNK, chunk, 0)


def _swa_prompt(attn, bias, gq2, gk2, sinks, *, batch, seq, tq):
    m = attn.shape[0]
    nt = seq // tq
    qw = N_HEADS_A * HEAD_DIM_A
    kvw = N_KV_A * HEAD_DIM_A
    row = lambda b, i: b * nt + i
    return pl.pallas_call(
        functools.partial(_swa_prompt_body, tq=tq),
        grid=(batch, nt),
        in_specs=[pl.BlockSpec(memory_space=pltpu.SMEM),
                  pl.BlockSpec((tq, qw), lambda b, i: (row(b, i), 0)),
                  pl.BlockSpec((tq, 2 * kvw), lambda b, i: (row(b, i), qw // (2 * kvw))),
                  _resident(bias.shape), _resident(gq2.shape), _resident(gk2.shape)],
        out_specs=[pl.BlockSpec((tq, qw), lambda b, i: (row(b, i), 0)),
                   pl.BlockSpec((1, WINDOW, kvw), lambda b, i: (b, 0, 0)),
                   pl.BlockSpec((1, WINDOW, kvw), lambda b, i: (b, 0, 0))],
        out_shape=[jax.ShapeDtypeStruct((m, qw), BF16),
                   jax.ShapeDtypeStruct((batch, WINDOW, kvw), F32),
                   jax.ShapeDtypeStruct((batch, WINDOW, kvw), F32)],
        scratch_shapes=[pltpu.VMEM((N_KV_A, WINDOW + tq, LANES), BF16),
                        pltpu.VMEM((N_KV_A, WINDOW + tq, LANES), BF16),
                        pltpu.VMEM((2, tq, qw), BF16)],
        compiler_params=_cparams("parallel", "arbitrary"),
        name="swa_prompt",
    )(sinks, attn, attn, bias, gq2, gk2)


def _swa_sample_body(sink_ref, q_ref, kv_ref, ck_ref, cv_ref, bias_ref, gq_ref, gk_ref,
                     o_ref, knew_ref, vnew_ref, *, tq):
    kvw = N_KV_A * HEAD_DIM_A
    lane = lax.broadcasted_iota(jnp.int32, (1, LANES), 1)
    lo_half = lane < HEAD_DIM_A
    bd = _block_diag_ones()
    scale = HEAD_DIM_A ** -0.5

    kk, vv = [None] * N_KV_A, [None] * N_KV_A
    for t in range(kvw // LANES):
        sl = slice(t * LANES, (t + 1) * LANES)
        k_t = _head_norm(kv_ref[:, sl].astype(F32), gk_ref[...], bd)
        v_t = kv_ref[:, kvw + t * LANES:kvw + (t + 1) * LANES].astype(F32)
        knew_ref[0, :, sl] = k_t
        vnew_ref[0, :, sl] = v_t
        k_all = jnp.concatenate([ck_ref[0, :, sl], k_t], axis=0)
        v_all = jnp.concatenate([cv_ref[0, :, sl], v_t], axis=0)
        k0, k1 = _dup_halves(k_all, lo_half)
        v0, v1 = _dup_halves(v_all, lo_half)
        kk[2 * t], kk[2 * t + 1] = k0.astype(BF16), k1.astype(BF16)
        vv[2 * t], vv[2 * t + 1] = v0.astype(BF16), v1.astype(BF16)

    qm = []
    for t in range(N_HEADS_A * HEAD_DIM_A // LANES):
        q_t = _head_norm(q_ref[:, t * LANES:(t + 1) * LANES].astype(F32), gq_ref[...], bd) * scale
        qm.append((jnp.where(lo_half, q_t, 0.0).astype(BF16), jnp.where(lo_half, 0.0, q_t).astype(BF16)))

    for j in range(N_KV_A):
        ta, tb = 2 * j, 2 * j + 1
        lhs = jnp.concatenate([qm[ta][0], qm[ta][1], qm[tb][0], qm[tb][1]], axis=0)
        out_a, out_b = _attend_group(lhs, kk[j], vv[j], bias_ref[j], _sink_col(sink_ref, j, tq), lo_half, tq)
        o_ref[:, ta * LANES:(ta + 1) * LANES] = out_a
        o_ref[:, tb * LANES:(tb + 1) * LANES] = out_b


def _swa_sample(attn, cache_k, cache_v, bias, gq2, gk2, sinks, *, batch, tq):
    m = attn.shape[0]
    qw = N_HEADS_A * HEAD_DIM_A
    kvw = N_KV_A * HEAD_DIM_A
    past = cache_k.shape[1]
    return pl.pallas_call(
        functools.partial(_swa_sample_body, tq=tq),
        grid=(batch,),
        in_specs=[pl.BlockSpec(memory_space=pltpu.SMEM),
                  pl.BlockSpec((tq, qw), lambda b: (b, 0)),
                  pl.BlockSpec((tq, 2 * kvw), lambda b: (b, qw // (2 * kvw))),
                  pl.BlockSpec((1, past, kvw), lambda b: (b, 0, 0)),
                  pl.BlockSpec((1, past, kvw), lambda b: (b, 0, 0)),
                  _resident(bias.shape), _resident(gq2.shape), _resident(gk2.shape)],
        out_specs=[pl.BlockSpec((tq, qw), lambda b: (b, 0)),
                   pl.BlockSpec((1, tq, kvw), lambda b: (b, 0, 0)),
                   pl.BlockSpec((1, tq, kvw), lambda b: (b, 0, 0))],
        out_shape=[jax.ShapeDtypeStruct((m, qw), BF16),
                   jax.ShapeDtypeStruct((batch, tq, kvw), F32),
                   jax.ShapeDtypeStruct((batch, tq, kvw), F32)],
        compiler_params=_cparams("parallel"),
        name="swa_sample",
    )(sinks, attn, attn, cache_k, cache_v, bias, gq2, gk2)


def _split3(x):
    b1 = x.astype(BF16)
    r1 = x - b1.astype(F32)
    b2 = r1.astype(BF16)
    b3 = (r1 - b2.astype(F32)).astype(BF16)
    return b1, b2, b3


def _gla_chunk(q, k, v, la, st_ref, o_store, cs):
    row = lax.broadcasted_iota(jnp.int32, (cs, cs), 0)
    col = lax.broadcasted_iota(jnp.int32, (cs, cs), 1)
    tri = jnp.where(col <= row, 1.0, 0.0).astype(BF16)
    b1, b2, b3 = _split3(la)
    cum = (jnp.dot(tri, b1, preferred_element_type=F32) + jnp.dot(tri, b2, preferred_element_type=F32)
           + jnp.dot(tri, b3, preferred_element_type=F32))
    last = cum[cs - 1:cs, :]
    scale = DK_B ** -0.5
    q_in = (q * (jnp.exp(cum) * scale)).astype(BF16)
    k_out = (k * jnp.exp(last - cum)).astype(BF16)
    dec = jnp.exp(last)

    nsb = cs // GLA_BLOCK
    qs, ks = [], []
    for sb in range(nsb):
        lo, hi = sb * GLA_BLOCK, (sb + 1) * GLA_BLOCK
        ref_row = cum[lo - 1:lo, :] if sb > 0 else jnp.zeros_like(last)
        qs.append((q[lo:hi] * (jnp.exp(cum[lo:hi] - ref_row) * scale)).astype(BF16))
        k_sb = (k[0:hi] * jnp.exp(ref_row - cum[0:hi])).astype(BF16)
        if hi < cs:
            k_sb = jnp.concatenate([k_sb, jnp.zeros((cs - hi, k.shape[1]), BF16)], axis=0)
        ks.append(k_sb)

    causal = col <= row
    nt = (((1,), (1,)), ((), ()))
    tn = (((0,), (0,)), ((), ()))
    for h in range(N_HEADS_B):
        ksl = slice(h * DK_B, (h + 1) * DK_B)
        vsl = slice(h * DV_B, (h + 1) * DV_B)
        att = jnp.concatenate(
            [lax.dot_general(qs[sb][:, ksl], ks[sb][:, ksl], nt, preferred_element_type=F32) for sb in range(nsb)],
            axis=0)
        att = jnp.where(causal, att, 0.0).astype(BF16)
        st = st_ref[h]
        o = (lax.dot_general(q_in[:, ksl], st.astype(BF16), nt, preferred_element_type=F32)
             + jnp.dot(att, v[:, vsl], preferred_element_type=F32))
        o_store(h, o)
        st_ref[h] = st * dec[:, ksl] + lax.dot_general(v[:, vsl], k_out[:, ksl], tn, preferred_element_type=F32)


def _gla_body(*refs, tq, cs, has_init):
    if has_init:
        qk_ref, v_ref, a1_ref, wa2_ref, ba_ref, s0_ref, o_ref, sout_ref, st_s, la_s = refs
    else:
        qk_ref, v_ref, a1_ref, wa2_ref, ba_ref, o_ref, sout_ref, st_s, la_s = refs
    i = pl.program_id(1)
    kw = N_HEADS_B * DK_B

    @pl.when(i == 0)
    def _():
        for h in range(N_HEADS_B):
            if has_init:
                st_s[h] = s0_ref[0, h].T
            else:
                st_s[h] = jnp.zeros((DV_B, DK_B), F32)

    z = jnp.dot(a1_ref[...], wa2_ref[...], preferred_element_type=F32) + ba_ref[...]
    la_s[...] = jax.nn.log_sigmoid(z) * (1.0 / GATE_TAU)

    def chunk(c, carry):
        r0 = pl.multiple_of(c * cs, cs)
        q = qk_ref[pl.ds(r0, cs), 0:kw].astype(F32)
        k = qk_ref[pl.ds(r0, cs), kw:2 * kw].astype(F32)
        v = v_ref[pl.ds(r0, cs), :]

        def o_store(h, o):
            o_ref[pl.ds(r0, cs), h * DV_B:(h + 1) * DV_B] = o.astype(BF16)

        _gla_chunk(q, k, v, la_s[pl.ds(r0, cs), :], st_s, o_store, cs)
        return carry

    lax.fori_loop(0, tq // cs, chunk, 0)

    @pl.when(i == pl.num_programs(1) - 1)
    def _():
        for h in range(N_HEADS_B):
            sout_ref[0, h] = st_s[h].T


def _gla(gla_in, wa2p, ba, s0, *, batch, seq, tq, cs):
    m = gla_in.shape[0]
    nt = seq // tq
    kw = N_HEADS_B * DK_B
    vw = N_HEADS_B * DV_B
    row = lambda b, i: b * nt + i
    has_init = s0 is not None
    in_specs = [pl.BlockSpec((tq, 2 * kw), lambda b, i: (row(b, i), 0)),
                pl.BlockSpec((tq, vw), lambda b, i: (row(b, i), 2 * kw // vw)),
                pl.BlockSpec((tq, A1_PAD), lambda b, i: (row(b, i), (2 * kw + vw) // A1_PAD)),
                _resident(wa2p.shape), _resident(ba.shape)]
    args = [gla_in, gla_in, gla_in, wa2p, ba]
    state_spec = pl.BlockSpec((1, N_HEADS_B, DK_B, DV_B), lambda b, i: (b, 0, 0, 0))
    if has_init:
        in_specs.append(state_spec)
        args.append(s0)
    return pl.pallas_call(
        functools.partial(_gla_body, tq=tq, cs=cs, has_init=has_init),
        grid=(batch, nt),
        in_specs=in_specs,
        out_specs=[pl.BlockSpec((tq, vw), lambda b, i: (row(b, i), 0)), state_spec],
        out_shape=[jax.ShapeDtypeStruct((m, vw), BF16),
                   jax.ShapeDtypeStruct((batch, N_HEADS_B, DK_B, DV_B), F32)],
        scratch_shapes=[pltpu.VMEM((N_HEADS_B, DV_B, DK_B), F32), pltpu.VMEM((tq, kw), F32)],
        compiler_params=_cparams("parallel", "arbitrary"),
        name="gla_init" if has_init else "gla",
    )(*args)


def _merge_ffn_body(x_ref, oa_ref, ob_ref, rb_ref, ga_ref, gb_ref, gob_ref, wo_ref, gf_ref,
                    wg_ref, wu_ref, wd_ref, y_ref, *, ff_chunk):
    ob = ob_ref[...].astype(F32)
    gob = gob_ref[...]
    obn = jnp.concatenate(
        [_rms_rows(ob[:, h * DV_B:(h + 1) * DV_B], gob) for h in range(N_HEADS_B)], axis=1)
    rb = rb_ref[...].astype(F32)
    obn = obn * (rb * jax.nn.sigmoid(rb))
    mix = (jax.nn.sigmoid(ga_ref[...].astype(F32)) * oa_ref[...].astype(F32)
           + jax.nn.sigmoid(gb_ref[...].astype(F32)) * obn).astype(BF16)
    x1 = x_ref[...] + jnp.dot(mix, wo_ref[...], preferred_element_type=F32)
    h = _rms_rows(x1, gf_ref[...]).astype(BF16)
    acc = x1
    f = wg_ref.shape[1]
    for c0 in range(0, f, ff_chunk):
        c1 = c0 + ff_chunk
        g = jnp.dot(h, wg_ref[:, c0:c1], preferred_element_type=F32)
        u = jnp.dot(h, wu_ref[:, c0:c1], preferred_element_type=F32)
        a = ((g * jax.nn.sigmoid(g)) * u).astype(BF16)
        acc = acc + jnp.dot(a, wd_ref[c0:c1, :], preferred_element_type=F32)
    y_ref[...] = acc


def _merge_ffn(x, oa, ob, gates, gob, wo, gf, wg, wu, wd, *, tm):
    m, d = x.shape
    f = wg.shape[1]
    ff_chunk = f // 2 if (f // 2) % LANES == 0 else f
    tile = lambda c: pl.BlockSpec((tm, d), lambda i: (i, c))
    return pl.pallas_call(
        functools.partial(_merge_ffn_body, ff_chunk=ff_chunk),
        grid=(m // tm,),
        in_specs=[tile(0), tile(0), tile(0), tile(0), tile(1), tile(2),
                  _resident(gob.shape), _resident(wo.shape), _resident(gf.shape),
                  _resident(wg.shape), _resident(wu.shape), _resident(wd.shape)],
        out_specs=tile(0),
        out_shape=jax.ShapeDtypeStruct((m, d), F32),
        compiler_params=_cparams("parallel"),
        name="merge_ffn",
    )(x, oa, ob, gates, gates, gates, gob, wo, gf, wg, wu, wd)


def _alibi_bias(cq, band, q_off):
    slopes = 2.0 ** (-8.0 * np.arange(1, N_HEADS_A + 1, dtype=np.float64) / N_HEADS_A)
    dist = np.abs(np.arange(cq)[:, None] + q_off - np.arange(band)[None, :])
    bias = -slopes.reshape(N_KV_A, GROUP_A, 1, 1) * dist[None, None]
    return jnp.asarray(bias.reshape(N_KV_A, GROUP_A * cq, band), F32)


def _row_tile(m, pref):
    t = min(pref, m)
    while m % t:
        t //= 2
    return t


def kernel(x_prompt, x_sample, cache_swa_k, cache_swa_v, state_gla, w_in, w_a2, b_a, g_q, g_k, sinks,
           g_ob, w_out, g_mix, g_ffn, w_gate, w_up, w_down):
    depth = w_in.shape[0]
    bp, tp, d = x_prompt.shape
    bs, ts, _ = x_sample.shape
    past = cache_swa_k.shape[2]
    qw, kvw = N_HEADS_A * HEAD_DIM_A, N_KV_A * HEAD_DIM_A
    kw, vw = N_HEADS_B * DK_B, N_HEADS_B * DV_B
    assert past == WINDOW and tp % CHUNK == 0 and ts % GLA_BLOCK == 0 and ts <= CHUNK

    sizes = (qw, kvw, kvw, kw, kw, vw, vw, GATE_RANK, d, d)
    offs = np.concatenate([[0], np.cumsum(sizes)])
    col = lambda a, b: w_in[:, :, int(offs[a]):int(offs[b])]
    w_attn = col(0, 3).astype(BF16)
    w_gla = jnp.concatenate(
        [col(3, 6), col(7, 8), jnp.zeros((depth, d, A1_PAD - GATE_RANK), w_in.dtype)], axis=2).astype(BF16)
    w_gates = jnp.concatenate([col(6, 7), col(8, 10)], axis=2).astype(BF16)
    w_a2p = jnp.concatenate(
        [w_a2, jnp.zeros((depth, A1_PAD - GATE_RANK, kw), w_a2.dtype)], axis=1).astype(BF16)
    w_out_b, w_gate_b, w_up_b, w_down_b = (w.astype(BF16) for w in (w_out, w_gate, w_up, w_down))
    gq2 = jnp.tile(g_q, (1, LANES // HEAD_DIM_A))[:, None, :]
    gk2 = jnp.tile(g_k, (1, LANES // HEAD_DIM_A))[:, None, :]
    gob4 = g_ob[:, None, :]
    bias_p = _alibi_bias(CHUNK, WINDOW + CHUNK, WINDOW)
    bias_s = _alibi_bias(ts, past + ts, past)

    tq_p = _row_tile(tp, 512)
    tm_p = _row_tile(bp * tp, 512)
    tm_s = _row_tile(bs * ts, 512)
    cs_p = CHUNK

    xp = x_prompt.reshape(bp * tp, d)
    xs = x_sample.reshape(bs * ts, d)
    pk, pv, ps, sk, sv, ss = [], [], [], [], [], []
    for l in range(depth):
        dense = dict(gob=gob4[l], wo=w_out_b[l], gf=g_ffn[l][None], wg=w_gate_b[l], wu=w_up_b[l], wd=w_down_b[l])
        attn, gla_in, gates = _inproj(xp, g_mix[l][None], w_attn[l], w_gla[l], w_gates[l], tm=tm_p)
        oa, kl, vl = _swa_prompt(attn, bias_p, gq2[l], gk2[l], sinks[l], batch=bp, seq=tp, tq=tq_p)
        ob, st = _gla(gla_in, w_a2p[l], b_a[l][None], None, batch=bp, seq=tp, tq=tq_p, cs=cs_p)
        xp = _merge_ffn(xp, oa, ob, gates, tm=tm_p, **dense)
        pk.append(kl.reshape(bp, WINDOW, N_KV_A, HEAD_DIM_A))
        pv.append(vl.reshape(bp, WINDOW, N_KV_A, HEAD_DIM_A))
        ps.append(st)
        attn, gla_in, gates = _inproj(xs, g_mix[l][None], w_attn[l], w_gla[l], w_gates[l], tm=tm_s)
        oa, kn, vn = _swa_sample(attn, cache_swa_k[l].reshape(bs, past, kvw), cache_swa_v[l].reshape(bs, past, kvw),
                                 bias_s, gq2[l], gk2[l], sinks[l], batch=bs, tq=ts)
        ob, st = _gla(gla_in, w_a2p[l], b_a[l][None], state_gla[l], batch=bs, seq=ts, tq=ts, cs=ts)
        xs = _merge_ffn(xs, oa, ob, gates, tm=tm_s, **dense)
        sk.append(kn.reshape(bs, ts, N_KV_A, HEAD_DIM_A))
        sv.append(vn.reshape(bs, ts, N_KV_A, HEAD_DIM_A))
        ss.append(st)

    return (xp.reshape(bp, tp, d), xs.reshape(bs, ts, d), jnp.stack(pk), jnp.stack(pv), jnp.stack(ps),
            jnp.stack(sk), jnp.stack(sv), jnp.stack(ss))
```

```python
import functools

import jax
import jax.numpy as jnp
import numpy as np
from jax import lax
from jax.experimental import pallas as pl
from jax.experimental.pallas import tpu as pltpu

CHUNK = 64
EPS = 1e-6
NEG = -1e30
N_HEADS_A = 16
N_KV_A = 4
GROUP_A = N_HEADS_A // N_KV_A
HEAD_DIM_A = 64
WINDOW = 128
N_HEADS_B = 4
DK_B = 128
DV_B = 256
GATE_RANK = 16
GATE_TAU = 16.0
GLA_BLOCK = 16

LANES = 128
A1_PAD = LANES
VMEM_LIMIT = 56 * 1024 * 1024
SCORE_RING = 6
PROB_RING = 4

BF16 = jnp.bfloat16
F32 = jnp.float32
NT = (((1,), (1,)), ((), ()))
TN = (((0,), (0,)), ((), ()))


def _cparams(*sem):
    return pltpu.CompilerParams(dimension_semantics=sem, vmem_limit_bytes=VMEM_LIMIT)


def _resident(shape):
    nd = len(shape)
    return pl.BlockSpec(shape, lambda *_: (0,) * nd, pipeline_mode=pl.Buffered(1))


def _rms_rows(x, g):
    ms = jnp.mean(x * x, axis=-1, keepdims=True)
    return (x * lax.rsqrt(ms + EPS)) * g


def _inproj_body(x_ref, g_ref, wa_ref, wb_ref, wc_ref, oa_ref, ob_ref, oc_ref, *, col_chunk):
    h = _rms_rows(x_ref[...], g_ref[...]).astype(BF16)
    for w_ref, o_ref in ((wa_ref, oa_ref), (wb_ref, ob_ref), (wc_ref, oc_ref)):
        n = w_ref.shape[1]
        for c0 in range(0, n, col_chunk):
            c1 = min(c0 + col_chunk, n)
            o_ref[:, c0:c1] = jnp.dot(h, w_ref[:, c0:c1], preferred_element_type=F32).astype(BF16)


def _inproj(x, g, wa, wb, wc, *, tm):
    m, d = x.shape
    outs = [jax.ShapeDtypeStruct((m, w.shape[1]), BF16) for w in (wa, wb, wc)]
    return pl.pallas_call(
        functools.partial(_inproj_body, col_chunk=512),
        grid=(m // tm,),
        in_specs=[pl.BlockSpec((tm, d), lambda i: (i, 0)), _resident(g.shape),
                  _resident(wa.shape), _resident(wb.shape), _resident(wc.shape)],
        out_specs=[pl.BlockSpec((tm, w.shape[1]), lambda i: (i, 0)) for w in (wa, wb, wc)],
        out_shape=outs,
        compiler_params=_cparams("parallel"),
        name="inproj",
    )(x, g, wa, wb, wc)


def _block_diag_mean():
    r = lax.broadcasted_iota(jnp.int32, (LANES, LANES), 0) // HEAD_DIM_A
    c = lax.broadcasted_iota(jnp.int32, (LANES, LANES), 1) // HEAD_DIM_A
    return jnp.where(r == c, 1.0 / HEAD_DIM_A, 0.0).astype(BF16)


def _head_unit(x, bdm):
    sq = x * x
    hi = sq.astype(BF16)
    lo = (sq - hi.astype(F32)).astype(BF16)
    ms = jnp.dot(hi, bdm, preferred_element_type=F32) + jnp.dot(lo, bdm, preferred_element_type=F32)
    return x * lax.rsqrt(ms + EPS)


def _dup_halves(x, lo_half):
    r = pltpu.roll(x, HEAD_DIM_A, axis=1)
    return jnp.where(lo_half, x, r), jnp.where(lo_half, r, x)


def _scores(lhs, kk, bias_t):
    return lax.dot_general(kk, lhs, NT, preferred_element_type=F32) + bias_t


def _sink_softmax(s, sink_row):
    m = jnp.maximum(jnp.max(s, axis=0, keepdims=True), sink_row)
    p = jnp.exp(s - m)
    l = jnp.sum(p, axis=0, keepdims=True) + jnp.exp(sink_row - m)
    return (p * (1.0 / l)).astype(BF16)


def _weighted_values(p, vv, lo_half, cq):
    o = lax.dot_general(p, vv, TN, preferred_element_type=F32)
    return jnp.where(lo_half, o[0:cq], o[cq:2 * cq]).astype(BF16)


def _attend_tile(lhs, kk, vv, bias_t, sink_row, lo_half, cq):
    return _weighted_values(_sink_softmax(_scores(lhs, kk, bias_t), sink_row), vv, lo_half, cq)


def _half_masks():
    lane = lax.broadcasted_iota(jnp.int32, (1, LANES), 1)
    lo_half = lane < HEAD_DIM_A
    return lo_half, jnp.where(lo_half, 1.0, 0.0), jnp.where(lo_half, 0.0, 1.0)


def _swa_prompt_body(q_ref, kv_ref, bias_ref, sink_ref, gq_ref, gk_ref,
                     o_ref, klast_ref, vlast_ref, kk_s, vv_s, qm_s, s_buf, p_buf, *, tq):
    i = pl.program_id(1)
    kvw = N_KV_A * HEAD_DIM_A
    nchunk = tq // CHUNK
    lo_half, mask_lo, mask_hi = _half_masks()
    bdm = _block_diag_mean()

    @pl.when(i == 0)
    def _():
        kk_s[:, 0:WINDOW, :] = jnp.zeros((N_KV_A, WINDOW, LANES), BF16)
        vv_s[:, 0:WINDOW, :] = jnp.zeros((N_KV_A, WINDOW, LANES), BF16)

    @pl.when(i > 0)
    def _():
        kk_s[:, 0:WINDOW, :] = kk_s[:, tq:tq + WINDOW, :]
        vv_s[:, 0:WINDOW, :] = vv_s[:, tq:tq + WINDOW, :]

    for t in range(kvw // LANES):
        k_t = _head_unit(kv_ref[:, t * LANES:(t + 1) * LANES].astype(F32), bdm) * gk_ref[...]
        v_t = kv_ref[:, kvw + t * LANES:kvw + (t + 1) * LANES].astype(F32)
        klast_ref[0, :, t * LANES:(t + 1) * LANES] = k_t[tq - WINDOW:, :]
        vlast_ref[0, :, t * LANES:(t + 1) * LANES] = v_t[tq - WINDOW:, :]
        k0, k1 = _dup_halves(k_t, lo_half)
        v0, v1 = _dup_halves(v_t, lo_half)
        kk_s[2 * t, WINDOW:, :] = k0.astype(BF16)
        kk_s[2 * t + 1, WINDOW:, :] = k1.astype(BF16)
        vv_s[2 * t, WINDOW:, :] = v0.astype(BF16)
        vv_s[2 * t + 1, WINDOW:, :] = v1.astype(BF16)

    gq = gq_ref[...] * (HEAD_DIM_A ** -0.5)
    gq_lo, gq_hi = gq * mask_lo, gq * mask_hi
    for t in range(N_HEADS_A * HEAD_DIM_A // LANES):
        q_t = _head_unit(q_ref[:, t * LANES:(t + 1) * LANES].astype(F32), bdm)
        qm_s[t, :, 0:CHUNK, :] = (q_t * gq_lo).astype(BF16).reshape(nchunk, CHUNK, LANES)
        qm_s[t, :, CHUNK:, :] = (q_t * gq_hi).astype(BF16).reshape(nchunk, CHUNK, LANES)

    band = WINDOW + CHUNK
    ntile = N_HEADS_A * HEAD_DIM_A // LANES
    nitem = nchunk * ntile
    n_sring, n_pring = s_buf.shape[0], p_buf.shape[0]
    skew_sm, skew_pv = n_sring - 2, n_pring - 1

    def keys(n):
        c, t = divmod(n, ntile)
        return t // 2, slice(c * CHUNK, c * CHUNK + band)

    def stage_scores(n):
        c, t = divmod(n, ntile)
        j, rows = keys(n)
        variant = jnp.where(i == 0, c + 1, 0) if c < WINDOW // CHUNK else 0
        s_buf[n % n_sring] = _scores(qm_s[t, c], kk_s[j, rows, :], bias_ref[variant, t])

    def stage_softmax(n):
        p_buf[n % n_pring] = _sink_softmax(s_buf[n % n_sring], sink_ref[n % ntile])

    def stage_values(n):
        c, t = divmod(n, ntile)
        j, rows = keys(n)
        o_ref[c * CHUNK:(c + 1) * CHUNK, t * LANES:(t + 1) * LANES] = _weighted_values(
            p_buf[n % n_pring], vv_s[j, rows, :], lo_half, CHUNK)

    for n in range(nitem + skew_sm + skew_pv):
        if n < nitem:
            stage_scores(n)
        if 0 <= n - skew_sm < nitem:
            stage_softmax(n - skew_sm)
        if 0 <= n - skew_sm - skew_pv < nitem:
            stage_values(n - skew_sm - skew_pv)


def _swa_prompt(attn, bias, sink_rows, gq2, gk2, *, batch, seq, tq):
    m = attn.shape[0]
    nt = seq // tq
    qw = N_HEADS_A * HEAD_DIM_A
    kvw = N_KV_A * HEAD_DIM_A
    row = lambda b, i: b * nt + i
    return pl.pallas_call(
        functools.partial(_swa_prompt_body, tq=tq),
        grid=(batch, nt),
        in_specs=[pl.BlockSpec((tq, qw), lambda b, i: (row(b, i), 0)),
                  pl.BlockSpec((tq, 2 * kvw), lambda b, i: (row(b, i), qw // (2 * kvw))),
                  _resident(bias.shape), _resident(sink_rows.shape),
                  _resident(gq2.shape), _resident(gk2.shape)],
        out_specs=[pl.BlockSpec((tq, qw), lambda b, i: (row(b, i), 0)),
                   pl.BlockSpec((1, WINDOW, kvw), lambda b, i: (b, 0, 0)),
                   pl.BlockSpec((1, WINDOW, kvw), lambda b, i: (b, 0, 0))],
        out_shape=[jax.ShapeDtypeStruct((m, qw), BF16),
                   jax.ShapeDtypeStruct((batch, WINDOW, kvw), F32),
                   jax.ShapeDtypeStruct((batch, WINDOW, kvw), F32)],
        scratch_shapes=[pltpu.VMEM((N_KV_A, WINDOW + tq, LANES), BF16),
                        pltpu.VMEM((N_KV_A, WINDOW + tq, LANES), BF16),
                        pltpu.VMEM((qw // LANES, tq // CHUNK, 2 * CHUNK, LANES), BF16),
                        pltpu.VMEM((SCORE_RING, WINDOW + CHUNK, LANES), F32),
                        pltpu.VMEM((PROB_RING, WINDOW + CHUNK, LANES), BF16)],
        compiler_params=_cparams("parallel", "arbitrary"),
        name="swa_prompt",
    )(attn, attn, bias, sink_rows, gq2, gk2)


def _swa_sample_body(q_ref, kv_ref, ck_ref, cv_ref, bias_ref, sink_ref, gq_ref, gk_ref,
                     o_ref, knew_ref, vnew_ref, *, tq):
    kvw = N_KV_A * HEAD_DIM_A
    lo_half, mask_lo, mask_hi = _half_masks()
    bdm = _block_diag_mean()

    kk, vv = [None] * N_KV_A, [None] * N_KV_A
    for t in range(kvw // LANES):
        sl = slice(t * LANES, (t + 1) * LANES)
        k_t = _head_unit(kv_ref[:, sl].astype(F32), bdm) * gk_ref[...]
        v_t = kv_ref[:, kvw + t * LANES:kvw + (t + 1) * LANES].astype(F32)
        knew_ref[0, :, sl] = k_t
        vnew_ref[0, :, sl] = v_t
        k0, k1 = _dup_halves(jnp.concatenate([ck_ref[0, :, sl], k_t], axis=0), lo_half)
        v0, v1 = _dup_halves(jnp.concatenate([cv_ref[0, :, sl], v_t], axis=0), lo_half)
        kk[2 * t], kk[2 * t + 1] = k0.astype(BF16), k1.astype(BF16)
        vv[2 * t], vv[2 * t + 1] = v0.astype(BF16), v1.astype(BF16)

    gq = gq_ref[...] * (HEAD_DIM_A ** -0.5)
    gq_lo, gq_hi = gq * mask_lo, gq * mask_hi
    for t in range(N_HEADS_A * HEAD_DIM_A // LANES):
        q_t = _head_unit(q_ref[:, t * LANES:(t + 1) * LANES].astype(F32), bdm)
        lhs = jnp.concatenate([(q_t * gq_lo).astype(BF16), (q_t * gq_hi).astype(BF16)], axis=0)
        o_ref[:, t * LANES:(t + 1) * LANES] = _attend_tile(
            lhs, kk[t // 2], vv[t // 2], bias_ref[0, t], sink_ref[t], lo_half, tq)


def _swa_sample(attn, cache_k, cache_v, bias, sink_rows, gq2, gk2, *, batch, tq):
    m = attn.shape[0]
    qw = N_HEADS_A * HEAD_DIM_A
    kvw = N_KV_A * HEAD_DIM_A
    past = cache_k.shape[1]
    return pl.pallas_call(
        functools.partial(_swa_sample_body, tq=tq),
        grid=(batch,),
        in_specs=[pl.BlockSpec((tq, qw), lambda b: (b, 0)),
                  pl.BlockSpec((tq, 2 * kvw), lambda b: (b, qw // (2 * kvw))),
                  pl.BlockSpec((1, past, kvw), lambda b: (b, 0, 0)),
                  pl.BlockSpec((1, past, kvw), lambda b: (b, 0, 0)),
                  _resident(bias.shape), _resident(sink_rows.shape),
                  _resident(gq2.shape), _resident(gk2.shape)],
        out_specs=[pl.BlockSpec((tq, qw), lambda b: (b, 0)),
                   pl.BlockSpec((1, tq, kvw), lambda b: (b, 0, 0)),
                   pl.BlockSpec((1, tq, kvw), lambda b: (b, 0, 0))],
        out_shape=[jax.ShapeDtypeStruct((m, qw), BF16),
                   jax.ShapeDtypeStruct((batch, tq, kvw), F32),
                   jax.ShapeDtypeStruct((batch, tq, kvw), F32)],
        compiler_params=_cparams("parallel"),
        name="swa_sample",
    )(attn, attn, cache_k, cache_v, bias, sink_rows, gq2, gk2)


def _log_sigmoid(z):
    return -(jnp.maximum(-z, 0.0) + jnp.log1p(jnp.exp(-jnp.abs(z))))


def _gla_body(*refs, tq, cs, has_init):
    if has_init:
        (qk_ref, v_ref, a1_ref, wa2_ref, ba_ref, s0_ref, o_ref, sout_ref,
         st_s, cum_s, qin_s, kout_s, qs_s, ks_s, dec_s, att_s, oi_s, upd_s) = refs
    else:
        (qk_ref, v_ref, a1_ref, wa2_ref, ba_ref, o_ref, sout_ref,
         st_s, cum_s, qin_s, kout_s, qs_s, ks_s, dec_s, att_s, oi_s, upd_s) = refs
    i = pl.program_id(1)
    kw = N_HEADS_B * DK_B
    nchunk = tq // cs
    nsb = cs // GLA_BLOCK
    scale = DK_B ** -0.5
    chunk_rows = [slice(c * cs, (c + 1) * cs) for c in range(nchunk)]
    ksls = [slice(h * DK_B, (h + 1) * DK_B) for h in range(N_HEADS_B)]
    vsls = [slice(h * DV_B, (h + 1) * DV_B) for h in range(N_HEADS_B)]

    @pl.when(i == 0)
    def _():
        for h in range(N_HEADS_B):
            if has_init:
                st_s[h] = s0_ref[0, h].T
            else:
                st_s[h] = jnp.zeros((DV_B, DK_B), F32)

    z = jnp.dot(a1_ref[...], wa2_ref[...], preferred_element_type=F32) + ba_ref[...]
    la = _log_sigmoid(z) * (1.0 / GATE_TAU)
    row = lax.broadcasted_iota(jnp.int32, (cs, cs), 0)
    col = lax.broadcasted_iota(jnp.int32, (cs, cs), 1)
    causal = col <= row
    tri = jnp.where(causal, 1.0, 0.0).astype(BF16)
    la_hi = la.astype(BF16)
    la_lo = (la - la_hi.astype(F32)).astype(BF16)
    for rows in chunk_rows:
        cum_s[rows, :] = (jnp.dot(tri, la_hi[rows], preferred_element_type=F32)
                          + jnp.dot(tri, la_lo[rows], preferred_element_type=F32))

    for c, rows in enumerate(chunk_rows):
        cum = cum_s[rows, :]
        q = qk_ref[rows, 0:kw].astype(F32)
        k = qk_ref[rows, kw:2 * kw].astype(F32)
        last = cum[cs - 1:cs, :]
        qin_s[rows, :] = (q * (jnp.exp(cum) * scale)).astype(BF16)
        kout_s[rows, :] = (k * jnp.exp(last - cum)).astype(BF16)
        dec_s[c] = jnp.exp(last)
        for sb in range(nsb):
            lo, hi = sb * GLA_BLOCK, (sb + 1) * GLA_BLOCK
            ref_row = cum[lo - 1:lo, :] if sb > 0 else jnp.zeros_like(last)
            qs_s[c * cs + lo:c * cs + hi, :] = (q[lo:hi] * (jnp.exp(cum[lo:hi] - ref_row) * scale)).astype(BF16)
            ks_s[sb, c * cs:c * cs + hi, :] = (k[0:hi] * jnp.exp(ref_row - cum[0:hi])).astype(BF16)
            if hi < cs:
                ks_s[sb, c * cs + hi:(c + 1) * cs, :] = jnp.zeros((cs - hi, kw), BF16)

    for c, rows in enumerate(chunk_rows):
        for h in range(N_HEADS_B):
            att = jnp.concatenate(
                [lax.dot_general(qs_s[c * cs + sb * GLA_BLOCK:c * cs + (sb + 1) * GLA_BLOCK, ksls[h]],
                                 ks_s[sb, rows, ksls[h]], NT, preferred_element_type=F32) for sb in range(nsb)],
                axis=0)
            att_s[c, h] = jnp.where(causal, att, 0.0).astype(BF16)
    for c, rows in enumerate(chunk_rows):
        for h in range(N_HEADS_B):
            oi_s[rows, vsls[h]] = jnp.dot(att_s[c, h], v_ref[rows, vsls[h]], preferred_element_type=F32)
            upd_s[c, h] = lax.dot_general(v_ref[rows, vsls[h]], kout_s[rows, ksls[h]], TN,
                                          preferred_element_type=F32)

    for c, rows in enumerate(chunk_rows):
        dec = dec_s[c]
        for h in range(N_HEADS_B):
            st = st_s[h]
            o = oi_s[rows, vsls[h]] + lax.dot_general(qin_s[rows, ksls[h]], st.astype(BF16), NT,
                                                      preferred_element_type=F32)
            o_ref[rows, vsls[h]] = o.astype(BF16)
            st_s[h] = st * dec[:, ksls[h]] + upd_s[c, h]

    @pl.when(i == pl.num_programs(1) - 1)
    def _():
        for h in range(N_HEADS_B):
            sout_ref[0, h] = st_s[h].T


def _gla(gla_in, wa2p, ba, s0, *, batch, seq, tq, cs):
    m = gla_in.shape[0]
    nt = seq // tq
    kw = N_HEADS_B * DK_B
    vw = N_HEADS_B * DV_B
    row = lambda b, i: b * nt + i
    has_init = s0 is not None
    in_specs = [pl.BlockSpec((tq, 2 * kw), lambda b, i: (row(b, i), 0)),
                pl.BlockSpec((tq, vw), lambda b, i: (row(b, i), 2 * kw // vw)),
                pl.BlockSpec((tq, A1_PAD), lambda b, i: (row(b, i), (2 * kw + vw) // A1_PAD)),
                _resident(wa2p.shape), _resident(ba.shape)]
    args = [gla_in, gla_in, gla_in, wa2p, ba]
    state_spec = pl.BlockSpec((1, N_HEADS_B, DK_B, DV_B), lambda b, i: (b, 0, 0, 0))
    if has_init:
        in_specs.append(state_spec)
        args.append(s0)
    nchunk = tq // cs
    return pl.pallas_call(
        functools.partial(_gla_body, tq=tq, cs=cs, has_init=has_init),
        grid=(batch, nt),
        in_specs=in_specs,
        out_specs=[pl.BlockSpec((tq, vw), lambda b, i: (row(b, i), 0)), state_spec],
        out_shape=[jax.ShapeDtypeStruct((m, vw), BF16),
                   jax.ShapeDtypeStruct((batch, N_HEADS_B, DK_B, DV_B), F32)],
        scratch_shapes=[pltpu.VMEM((N_HEADS_B, DV_B, DK_B), F32),
                        pltpu.VMEM((tq, kw), F32),
                        pltpu.VMEM((tq, kw), BF16),
                        pltpu.VMEM((tq, kw), BF16),
                        pltpu.VMEM((tq, kw), BF16),
                        pltpu.VMEM((cs // GLA_BLOCK, tq, kw), BF16),
                        pltpu.VMEM((nchunk, 1, kw), F32),
                        pltpu.VMEM((nchunk, N_HEADS_B, cs, cs), BF16),
                        pltpu.VMEM((tq, vw), F32),
                        pltpu.VMEM((nchunk, N_HEADS_B, DV_B, DK_B), F32)],
        compiler_params=_cparams("parallel", "arbitrary"),
        name="gla_init" if has_init else "gla",
    )(*args)


def _merge_ffn_body(x_ref, oa_ref, ob_ref, rb_ref, ga_ref, gb_ref, gob_ref, wo_ref, gf_ref,
                    wg_ref, wu_ref, wd_ref, y_ref, *, ff_chunk):
    ob = ob_ref[...].astype(F32)
    gob = gob_ref[...]
    obn = jnp.concatenate(
        [_rms_rows(ob[:, h * DV_B:(h + 1) * DV_B], gob) for h in range(N_HEADS_B)], axis=1)
    rb = rb_ref[...].astype(F32)
    obn = obn * (rb * jax.nn.sigmoid(rb))
    mix = (jax.nn.sigmoid(ga_ref[...].astype(F32)) * oa_ref[...].astype(F32)
           + jax.nn.sigmoid(gb_ref[...].astype(F32)) * obn).astype(BF16)
    x1 = x_ref[...] + jnp.dot(mix, wo_ref[...], preferred_element_type=F32)
    h = _rms_rows(x1, gf_ref[...]).astype(BF16)
    acc = x1
    f = wg_ref.shape[1]
    for c0 in range(0, f, ff_chunk):
        c1 = c0 + ff_chunk
        g = jnp.dot(h, wg_ref[:, c0:c1], preferred_element_type=F32)
        u = jnp.dot(h, wu_ref[:, c0:c1], preferred_element_type=F32)
        a = ((g * jax.nn.sigmoid(g)) * u).astype(BF16)
        acc = acc + jnp.dot(a, wd_ref[c0:c1, :], preferred_element_type=F32)
    y_ref[...] = acc


def _merge_ffn(x, oa, ob, gates, gob, wo, gf, wg, wu, wd, *, tm):
    m, d = x.shape
    f = wg.shape[1]
    ff_chunk = f // 2 if (f // 2) % LANES == 0 else f
    tile = lambda c: pl.BlockSpec((tm, d), lambda i: (i, c))
    return pl.pallas_call(
        functools.partial(_merge_ffn_body, ff_chunk=ff_chunk),
        grid=(m // tm,),
        in_specs=[tile(0), tile(0), tile(0), tile(0), tile(1), tile(2),
                  _resident(gob.shape), _resident(wo.shape), _resident(gf.shape),
                  _resident(wg.shape), _resident(wu.shape), _resident(wd.shape)],
        out_specs=tile(0),
        out_shape=jax.ShapeDtypeStruct((m, d), F32),
        compiler_params=_cparams("parallel"),
        name="merge_ffn",
    )(x, oa, ob, gates, gates, gates, gob, wo, gf, wg, wu, wd)


def _alibi_bias_t(cq, band, q_off, masked_prefixes):
    slopes = 2.0 ** (-8.0 * np.arange(1, N_HEADS_A + 1, dtype=np.float64) / N_HEADS_A)
    dist = np.abs(np.arange(band)[:, None] - q_off - np.arange(cq)[None, :])
    bias = -slopes.reshape(N_HEADS_A // 2, 1, 2, 1) * dist[None, :, None, :]
    bias = bias.reshape(N_HEADS_A // 2, band, 2 * cq)
    out = np.stack([np.where(np.arange(band)[None, :, None] < p, NEG, bias) for p in masked_prefixes])
    return jnp.asarray(out, F32)


def _sink_rows(sinks, cq):
    depth = sinks.shape[0]
    return jnp.repeat(sinks.reshape(depth, N_HEADS_A // 2, 2), cq, axis=2)[:, :, None, :]


def _row_tile(m, pref):
    t = min(pref, m)
    while m % t:
        t //= 2
    return t


def kernel(x_prompt, x_sample, cache_swa_k, cache_swa_v, state_gla, w_in, w_a2, b_a, g_q, g_k, sinks,
           g_ob, w_out, g_mix, g_ffn, w_gate, w_up, w_down):
    depth = w_in.shape[0]
    bp, tp, d = x_prompt.shape
    bs, ts, _ = x_sample.shape
    past = cache_swa_k.shape[2]
    qw, kvw = N_HEADS_A * HEAD_DIM_A, N_KV_A * HEAD_DIM_A
    kw, vw = N_HEADS_B * DK_B, N_HEADS_B * DV_B
    assert past == WINDOW and tp % CHUNK == 0 and ts % GLA_BLOCK == 0 and ts <= CHUNK

    sizes = (qw, kvw, kvw, kw, kw, vw, vw, GATE_RANK, d, d)
    offs = np.concatenate([[0], np.cumsum(sizes)])
    col = lambda a, b: w_in[:, :, int(offs[a]):int(offs[b])]
    w_attn = col(0, 3).astype(BF16)
    w_gla = jnp.concatenate(
        [col(3, 6), col(7, 8), jnp.zeros((depth, d, A1_PAD - GATE_RANK), w_in.dtype)], axis=2).astype(BF16)
    w_gates = jnp.concatenate([col(6, 7), col(8, 10)], axis=2).astype(BF16)
    w_a2p = jnp.concatenate(
        [w_a2, jnp.zeros((depth, A1_PAD - GATE_RANK, kw), w_a2.dtype)], axis=1).astype(BF16)
    w_out_b, w_gate_b, w_up_b, w_down_b = (w.astype(BF16) for w in (w_out, w_gate, w_up, w_down))
    gq2 = jnp.tile(g_q, (1, LANES // HEAD_DIM_A))[:, None, :]
    gk2 = jnp.tile(g_k, (1, LANES // HEAD_DIM_A))[:, None, :]
    gob4 = g_ob[:, None, :]
    bias_p = _alibi_bias_t(CHUNK, WINDOW + CHUNK, WINDOW, (0, WINDOW, WINDOW - CHUNK))
    bias_s = _alibi_bias_t(ts, past + ts, past, (0,))
    sink_p = _sink_rows(sinks, CHUNK)
    sink_s = _sink_rows(sinks, ts)

    tq_p = _row_tile(tp, 512)
    tm_p = _row_tile(bp * tp, 512)
    tm_s = _row_tile(bs * ts, 512)

    xp = x_prompt.reshape(bp * tp, d)
    xs = x_sample.reshape(bs * ts, d)
    pk, pv, ps, sk, sv, ss = [], [], [], [], [], []
    for l in range(depth):
        dense = dict(gob=gob4[l], wo=w_out_b[l], gf=g_ffn[l][None], wg=w_gate_b[l], wu=w_up_b[l], wd=w_down_b[l])
        attn, gla_in, gates = _inproj(xp, g_mix[l][None], w_attn[l], w_gla[l], w_gates[l], tm=tm_p)
        oa, kl, vl = _swa_prompt(attn, bias_p, sink_p[l], gq2[l], gk2[l], batch=bp, seq=tp, tq=tq_p)
        ob, st = _gla(gla_in, w_a2p[l], b_a[l][None], None, batch=bp, seq=tp, tq=tq_p, cs=CHUNK)
        xp = _merge_ffn(xp, oa, ob, gates, tm=tm_p, **dense)
        pk.append(kl.reshape(bp, WINDOW, N_KV_A, HEAD_DIM_A))
        pv.append(vl.reshape(bp, WINDOW, N_KV_A, HEAD_DIM_A))
        ps.append(st)
        attn, gla_in, gates = _inproj(xs, g_mix[l][None], w_attn[l], w_gla[l], w_gates[l], tm=tm_s)
        oa, kn, vn = _swa_sample(attn, cache_swa_k[l].reshape(bs, past, kvw), cache_swa_v[l].reshape(bs, past, kvw),
                                 bias_s, sink_s[l], gq2[l], gk2[l], batch=bs, tq=ts)
        ob, st = _gla(gla_in, w_a2p[l], b_a[l][None], state_gla[l], batch=bs, seq=ts, tq=ts, cs=ts)
        xs = _merge_ffn(xs, oa, ob, gates, tm=tm_s, **dense)
        sk.append(kn.reshape(bs, ts, N_KV_A, HEAD_DIM_A))
        sv.append(vn.reshape(bs, ts, N_KV_A, HEAD_DIM_A))
        ss.append(st)

    return (xp.reshape(bp, tp, d), xs.reshape(bs, ts, d), jnp.stack(pk), jnp.stack(pv), jnp.stack(ps),
            jnp.stack(sk), jnp.stack(sv), jnp.stack(ss))
```

```python
import functools

import jax
import jax.numpy as jnp
import numpy as np
from jax import lax
from jax.experimental import pallas as pl
from jax.experimental.pallas import tpu as pltpu

CHUNK = 64
EPS = 1e-6
NEG = -1e30
N_HEADS_A = 16
N_KV_A = 4
GROUP_A = N_HEADS_A // N_KV_A
HEAD_DIM_A = 64
WINDOW = 128
N_HEADS_B = 4
DK_B = 128
DV_B = 256
GATE_RANK = 16
GATE_TAU = 16.0
GLA_BLOCK = 16

LANES = 128
A1_PAD = LANES
VMEM_LIMIT = 56 * 1024 * 1024
SCORE_RING = 6
PROB_RING = 4
DOWN_LAG = 2
LOG2E = 1.4426950408889634

BF16 = jnp.bfloat16
F32 = jnp.float32
NT = (((1,), (1,)), ((), ()))
TN = (((0,), (0,)), ((), ()))


def _cparams(*sem):
    return pltpu.CompilerParams(dimension_semantics=sem, vmem_limit_bytes=VMEM_LIMIT)


def _resident(shape):
    nd = len(shape)
    return pl.BlockSpec(shape, lambda *_: (0,) * nd, pipeline_mode=pl.Buffered(1))


def _rms_rows(x, g):
    ms = jnp.mean(x * x, axis=-1, keepdims=True)
    return (x * lax.rsqrt(ms + EPS)) * g


def _head_norm_tile(x, g2, lo_half):
    sq = x * x
    ms_lo = jnp.sum(jnp.where(lo_half, sq, 0.0), axis=-1, keepdims=True) * (1.0 / HEAD_DIM_A)
    ms_hi = jnp.sum(jnp.where(lo_half, 0.0, sq), axis=-1, keepdims=True) * (1.0 / HEAD_DIM_A)
    r = jnp.where(lo_half, lax.rsqrt(ms_lo + EPS), lax.rsqrt(ms_hi + EPS))
    return (x * r) * g2


def _inproj_body(x_ref, g_ref, gq_ref, gk_ref, wa_ref, wb_ref, wc_ref, oa_ref, ob_ref, oc_ref, *, col_chunk):
    h = _rms_rows(x_ref[...], g_ref[...]).astype(BF16)
    lo_half = lax.broadcasted_iota(jnp.int32, (1, LANES), 1) < HEAD_DIM_A
    qw, kvw = N_HEADS_A * HEAD_DIM_A, N_KV_A * HEAD_DIM_A
    for c0 in range(0, wa_ref.shape[1], col_chunk):
        acc = jnp.dot(h, wa_ref[:, c0:c0 + col_chunk], preferred_element_type=F32)
        for t0 in range(0, col_chunk, LANES):
            col = c0 + t0
            tile = acc[:, t0:t0 + LANES]
            if col < qw:
                tile = _head_norm_tile(tile, gq_ref[...], lo_half)
            elif col < qw + kvw:
                tile = _head_norm_tile(tile, gk_ref[...], lo_half)
            oa_ref[:, col:col + LANES] = tile.astype(BF16)
    for w_ref, o_ref in ((wb_ref, ob_ref), (wc_ref, oc_ref)):
        n = w_ref.shape[1]
        for c0 in range(0, n, col_chunk):
            c1 = min(c0 + col_chunk, n)
            o_ref[:, c0:c1] = jnp.dot(h, w_ref[:, c0:c1], preferred_element_type=F32).astype(BF16)


def _inproj(x, g, gq2, gk2, wa, wb, wc, *, tm):
    m, d = x.shape
    outs = [jax.ShapeDtypeStruct((m, w.shape[1]), BF16) for w in (wa, wb, wc)]
    return pl.pallas_call(
        functools.partial(_inproj_body, col_chunk=512),
        grid=(m // tm,),
        in_specs=[pl.BlockSpec((tm, d), lambda i: (i, 0)), _resident(g.shape),
                  _resident(gq2.shape), _resident(gk2.shape),
                  _resident(wa.shape), _resident(wb.shape), _resident(wc.shape)],
        out_specs=[pl.BlockSpec((tm, w.shape[1]), lambda i: (i, 0)) for w in (wa, wb, wc)],
        out_shape=outs,
        compiler_params=_cparams("parallel"),
        name="inproj",
    )(x, g, gq2, gk2, wa, wb, wc)


def _dup_halves(x, lo_half):
    r = pltpu.roll(x, HEAD_DIM_A, axis=1)
    return jnp.where(lo_half, x, r), jnp.where(lo_half, r, x)


def _scores(lhs, kk, bias_t):
    return lax.dot_general(kk, lhs, NT, preferred_element_type=F32) + bias_t


def _sink_softmax(s, sink_row):
    m = jnp.maximum(jnp.max(s, axis=0, keepdims=True), sink_row)
    p = jnp.exp2(s - m)
    l = jnp.sum(p, axis=0, keepdims=True) + jnp.exp2(sink_row - m)
    return (p * (1.0 / l)).astype(BF16)


def _weighted_values(p, vv, lo_half, cq):
    o = lax.dot_general(p, vv, TN, preferred_element_type=F32)
    return jnp.where(lo_half, o[0:cq], o[cq:2 * cq]).astype(BF16)


def _attend_tile(lhs, kk, vv, bias_t, sink_row, lo_half, cq):
    return _weighted_values(_sink_softmax(_scores(lhs, kk, bias_t), sink_row), vv, lo_half, cq)


def _half_masks():
    lane = lax.broadcasted_iota(jnp.int32, (1, LANES), 1)
    lo_half = lane < HEAD_DIM_A
    return lo_half, jnp.where(lo_half, 1.0, 0.0).astype(BF16), jnp.where(lo_half, 0.0, 1.0).astype(BF16)


def _swa_prompt_body(q_ref, kv_ref, bias_ref, sink_ref,
                     o_ref, klast_ref, vlast_ref, kk_s, vv_s, qm_s, s_buf, p_buf, *, tq):
    i = pl.program_id(1)
    kvw = N_KV_A * HEAD_DIM_A
    nchunk = tq // CHUNK
    lo_half, mask_lo, mask_hi = _half_masks()

    @pl.when(i == 0)
    def _():
        kk_s[:, 0:WINDOW, :] = jnp.zeros((N_KV_A, WINDOW, LANES), BF16)
        vv_s[:, 0:WINDOW, :] = jnp.zeros((N_KV_A, WINDOW, LANES), BF16)

    @pl.when(i > 0)
    def _():
        kk_s[:, 0:WINDOW, :] = kk_s[:, tq:tq + WINDOW, :]
        vv_s[:, 0:WINDOW, :] = vv_s[:, tq:tq + WINDOW, :]

    for t in range(kvw // LANES):
        k_t = kv_ref[:, t * LANES:(t + 1) * LANES].astype(F32)
        v_t = kv_ref[:, kvw + t * LANES:kvw + (t + 1) * LANES].astype(F32)
        klast_ref[0, :, t * LANES:(t + 1) * LANES] = k_t[tq - WINDOW:, :]
        vlast_ref[0, :, t * LANES:(t + 1) * LANES] = v_t[tq - WINDOW:, :]
        k0, k1 = _dup_halves(k_t, lo_half)
        v0, v1 = _dup_halves(v_t, lo_half)
        kk_s[2 * t, WINDOW:, :] = k0.astype(BF16)
        kk_s[2 * t + 1, WINDOW:, :] = k1.astype(BF16)
        vv_s[2 * t, WINDOW:, :] = v0.astype(BF16)
        vv_s[2 * t + 1, WINDOW:, :] = v1.astype(BF16)

    for t in range(N_HEADS_A * HEAD_DIM_A // LANES):
        q_t = q_ref[:, t * LANES:(t + 1) * LANES]
        qm_s[t, :, 0:CHUNK, :] = (q_t * mask_lo).reshape(nchunk, CHUNK, LANES)
        qm_s[t, :, CHUNK:, :] = (q_t * mask_hi).reshape(nchunk, CHUNK, LANES)

    band = WINDOW + CHUNK
    ntile = N_HEADS_A * HEAD_DIM_A // LANES
    nitem = nchunk * ntile
    n_sring, n_pring = s_buf.shape[0], p_buf.shape[0]
    skew_sm, skew_pv = n_sring - 2, n_pring - 1

    def keys(n):
        c, t = divmod(n, ntile)
        return t // 2, slice(c * CHUNK, c * CHUNK + band)

    def stage_scores(n):
        c, t = divmod(n, ntile)
        j, rows = keys(n)
        variant = jnp.where(i == 0, c + 1, 0) if c < WINDOW // CHUNK else 0
        s_buf[n % n_sring] = _scores(qm_s[t, c], kk_s[j, rows, :], bias_ref[variant, t])

    def stage_softmax(n):
        p_buf[n % n_pring] = _sink_softmax(s_buf[n % n_sring], sink_ref[n % ntile])

    def stage_values(n):
        c, t = divmod(n, ntile)
        j, rows = keys(n)
        o_ref[c * CHUNK:(c + 1) * CHUNK, t * LANES:(t + 1) * LANES] = _weighted_values(
            p_buf[n % n_pring], vv_s[j, rows, :], lo_half, CHUNK)

    for n in range(nitem + skew_sm + skew_pv):
        if n < nitem:
            stage_scores(n)
        if 0 <= n - skew_sm < nitem:
            stage_softmax(n - skew_sm)
        if 0 <= n - skew_sm - skew_pv < nitem:
            stage_values(n - skew_sm - skew_pv)


def _swa_prompt(attn, bias, sink_rows, *, batch, seq, tq):
    m = attn.shape[0]
    nt = seq // tq
    qw = N_HEADS_A * HEAD_DIM_A
    kvw = N_KV_A * HEAD_DIM_A
    row = lambda b, i: b * nt + i
    return pl.pallas_call(
        functools.partial(_swa_prompt_body, tq=tq),
        grid=(batch, nt),
        in_specs=[pl.BlockSpec((tq, qw), lambda b, i: (row(b, i), 0)),
                  pl.BlockSpec((tq, 2 * kvw), lambda b, i: (row(b, i), qw // (2 * kvw))),
                  _resident(bias.shape), _resident(sink_rows.shape)],
        out_specs=[pl.BlockSpec((tq, qw), lambda b, i: (row(b, i), 0)),
                   pl.BlockSpec((1, WINDOW, kvw), lambda b, i: (b, 0, 0)),
                   pl.BlockSpec((1, WINDOW, kvw), lambda b, i: (b, 0, 0))],
        out_shape=[jax.ShapeDtypeStruct((m, qw), BF16),
                   jax.ShapeDtypeStruct((batch, WINDOW, kvw), F32),
                   jax.ShapeDtypeStruct((batch, WINDOW, kvw), F32)],
        scratch_shapes=[pltpu.VMEM((N_KV_A, WINDOW + tq, LANES), BF16),
                        pltpu.VMEM((N_KV_A, WINDOW + tq, LANES), BF16),
                        pltpu.VMEM((qw // LANES, tq // CHUNK, 2 * CHUNK, LANES), BF16),
                        pltpu.VMEM((SCORE_RING, WINDOW + CHUNK, LANES), F32),
                        pltpu.VMEM((PROB_RING, WINDOW + CHUNK, LANES), BF16)],
        compiler_params=_cparams("parallel", "arbitrary"),
        name="swa_prompt",
    )(attn, attn, bias, sink_rows)


def _swa_sample_body(q_ref, kv_ref, ck_ref, cv_ref, bias_ref, sink_ref,
                     o_ref, knew_ref, vnew_ref, *, tq, nb):
    kvw = N_KV_A * HEAD_DIM_A
    lo_half, mask_lo, mask_hi = _half_masks()
    for b in range(nb):
        rows = slice(b * tq, (b + 1) * tq)
        kk, vv = [None] * N_KV_A, [None] * N_KV_A
        for t in range(kvw // LANES):
            sl = slice(t * LANES, (t + 1) * LANES)
            k_t = kv_ref[rows, sl].astype(F32)
            v_t = kv_ref[rows, kvw + t * LANES:kvw + (t + 1) * LANES].astype(F32)
            knew_ref[b, :, sl] = k_t
            vnew_ref[b, :, sl] = v_t
            k0, k1 = _dup_halves(jnp.concatenate([ck_ref[b, :, sl], k_t], axis=0), lo_half)
            v0, v1 = _dup_halves(jnp.concatenate([cv_ref[b, :, sl], v_t], axis=0), lo_half)
            kk[2 * t], kk[2 * t + 1] = k0.astype(BF16), k1.astype(BF16)
            vv[2 * t], vv[2 * t + 1] = v0.astype(BF16), v1.astype(BF16)
        for t in range(N_HEADS_A * HEAD_DIM_A // LANES):
            q_t = q_ref[rows, t * LANES:(t + 1) * LANES]
            lhs = jnp.concatenate([q_t * mask_lo, q_t * mask_hi], axis=0)
            o_ref[rows, t * LANES:(t + 1) * LANES] = _attend_tile(
                lhs, kk[t // 2], vv[t // 2], bias_ref[0, t], sink_ref[t], lo_half, tq)


def _swa_sample(attn, cache_k, cache_v, bias, sink_rows, *, batch, tq, nb):
    m = attn.shape[0]
    qw = N_HEADS_A * HEAD_DIM_A
    kvw = N_KV_A * HEAD_DIM_A
    past = cache_k.shape[1]
    return pl.pallas_call(
        functools.partial(_swa_sample_body, tq=tq, nb=nb),
        grid=(batch // nb,),
        in_specs=[pl.BlockSpec((nb * tq, qw), lambda b: (b, 0)),
                  pl.BlockSpec((nb * tq, 2 * kvw), lambda b: (b, qw // (2 * kvw))),
                  pl.BlockSpec((nb, past, kvw), lambda b: (b, 0, 0)),
                  pl.BlockSpec((nb, past, kvw), lambda b: (b, 0, 0)),
                  _resident(bias.shape), _resident(sink_rows.shape)],
        out_specs=[pl.BlockSpec((nb * tq, qw), lambda b: (b, 0)),
                   pl.BlockSpec((nb, tq, kvw), lambda b: (b, 0, 0)),
                   pl.BlockSpec((nb, tq, kvw), lambda b: (b, 0, 0))],
        out_shape=[jax.ShapeDtypeStruct((m, qw), BF16),
                   jax.ShapeDtypeStruct((batch, tq, kvw), F32),
                   jax.ShapeDtypeStruct((batch, tq, kvw), F32)],
        compiler_params=_cparams("parallel"),
        name="swa_sample",
    )(attn, attn, cache_k, cache_v, bias, sink_rows)


def _log_sigmoid(z):
    return -(jnp.maximum(-z, 0.0) + jnp.log1p(jnp.exp(-jnp.abs(z))))


def _gla_body(*refs, tq, cs, has_init):
    if has_init:
        (qk_ref, v_ref, a1_ref, wa2_ref, ba_ref, s0_ref, o_ref, sout_ref,
         st_s, cum_s, qin_s, kout_s, qs_s, ks_s, dec_s, att_s, oi_s, upd_s) = refs
    else:
        (qk_ref, v_ref, a1_ref, wa2_ref, ba_ref, o_ref, sout_ref,
         st_s, cum_s, qin_s, kout_s, qs_s, ks_s, dec_s, att_s, oi_s, upd_s) = refs
    i = pl.program_id(1)
    kw = N_HEADS_B * DK_B
    nchunk = tq // cs
    nsb = cs // GLA_BLOCK
    scale = DK_B ** -0.5
    chunk_rows = [slice(c * cs, (c + 1) * cs) for c in range(nchunk)]
    ksls = [slice(h * DK_B, (h + 1) * DK_B) for h in range(N_HEADS_B)]
    vsls = [slice(h * DV_B, (h + 1) * DV_B) for h in range(N_HEADS_B)]

    @pl.when(i == 0)
    def _():
        for h in range(N_HEADS_B):
            if has_init:
                st_s[h] = s0_ref[0, h].T
            else:
                st_s[h] = jnp.zeros((DV_B, DK_B), F32)

    z = jnp.dot(a1_ref[...], wa2_ref[...], preferred_element_type=F32) + ba_ref[...]
    la = _log_sigmoid(z) * (1.0 / GATE_TAU)
    row = lax.broadcasted_iota(jnp.int32, (cs, cs), 0)
    col = lax.broadcasted_iota(jnp.int32, (cs, cs), 1)
    causal = col <= row
    tri = jnp.where(causal, 1.0, 0.0).astype(BF16)
    la_hi = la.astype(BF16)
    la_lo = (la - la_hi.astype(F32)).astype(BF16)
    for rows in chunk_rows:
        cum_s[rows, :] = (jnp.dot(tri, la_hi[rows], preferred_element_type=F32)
                          + jnp.dot(tri, la_lo[rows], preferred_element_type=F32))

    for c, rows in enumerate(chunk_rows):
        cum = cum_s[rows, :]
        q = qk_ref[rows, 0:kw].astype(F32)
        k = qk_ref[rows, kw:2 * kw].astype(F32)
        last = cum[cs - 1:cs, :]
        qin_s[rows, :] = (q * (jnp.exp(cum) * scale)).astype(BF16)
        kout_s[rows, :] = (k * jnp.exp(last - cum)).astype(BF16)
        dec_s[c] = jnp.exp(last)
        for sb in range(nsb):
            lo, hi = sb * GLA_BLOCK, (sb + 1) * GLA_BLOCK
            ref_row = cum[lo - 1:lo, :] if sb > 0 else jnp.zeros_like(last)
            qs_s[c * cs + lo:c * cs + hi, :] = (q[lo:hi] * (jnp.exp(cum[lo:hi] - ref_row) * scale)).astype(BF16)
            ks_s[sb, c * cs:c * cs + hi, :] = (k[0:hi] * jnp.exp(ref_row - cum[0:hi])).astype(BF16)
            if hi < cs:
                ks_s[sb, c * cs + hi:(c + 1) * cs, :] = jnp.zeros((cs - hi, kw), BF16)

    for c, rows in enumerate(chunk_rows):
        for h in range(N_HEADS_B):
            att = jnp.concatenate(
                [lax.dot_general(qs_s[c * cs + sb * GLA_BLOCK:c * cs + (sb + 1) * GLA_BLOCK, ksls[h]],
                                 ks_s[sb, rows, ksls[h]], NT, preferred_element_type=F32) for sb in range(nsb)],
                axis=0)
            att_s[c, h] = jnp.where(causal, att, 0.0).astype(BF16)
    for c, rows in enumerate(chunk_rows):
        for h in range(N_HEADS_B):
            oi_s[rows, vsls[h]] = jnp.dot(att_s[c, h], v_ref[rows, vsls[h]], preferred_element_type=F32)
            upd_s[c, h] = lax.dot_general(v_ref[rows, vsls[h]], kout_s[rows, ksls[h]], TN,
                                          preferred_element_type=F32)

    for c, rows in enumerate(chunk_rows):
        dec = dec_s[c]
        for h in range(N_HEADS_B):
            st = st_s[h]
            o = oi_s[rows, vsls[h]] + lax.dot_general(qin_s[rows, ksls[h]], st.astype(BF16), NT,
                                                      preferred_element_type=F32)
            o_ref[rows, vsls[h]] = o.astype(BF16)
            st_s[h] = st * dec[:, ksls[h]] + upd_s[c, h]

    @pl.when(i == pl.num_programs(1) - 1)
    def _():
        for h in range(N_HEADS_B):
            sout_ref[0, h] = st_s[h].T


def _gla(gla_in, wa2p, ba, s0, *, batch, seq, tq, cs):
    m = gla_in.shape[0]
    nt = seq // tq
    kw = N_HEADS_B * DK_B
    vw = N_HEADS_B * DV_B
    row = lambda b, i: b * nt + i
    has_init = s0 is not None
    in_specs = [pl.BlockSpec((tq, 2 * kw), lambda b, i: (row(b, i), 0)),
                pl.BlockSpec((tq, vw), lambda b, i: (row(b, i), 2 * kw // vw)),
                pl.BlockSpec((tq, A1_PAD), lambda b, i: (row(b, i), (2 * kw + vw) // A1_PAD)),
                _resident(wa2p.shape), _resident(ba.shape)]
    args = [gla_in, gla_in, gla_in, wa2p, ba]
    state_spec = pl.BlockSpec((1, N_HEADS_B, DK_B, DV_B), lambda b, i: (b, 0, 0, 0))
    if has_init:
        in_specs.append(state_spec)
        args.append(s0)
    nchunk = tq // cs
    return pl.pallas_call(
        functools.partial(_gla_body, tq=tq, cs=cs, has_init=has_init),
        grid=(batch, nt),
        in_specs=in_specs,
        out_specs=[pl.BlockSpec((tq, vw), lambda b, i: (row(b, i), 0)), state_spec],
        out_shape=[jax.ShapeDtypeStruct((m, vw), BF16),
                   jax.ShapeDtypeStruct((batch, N_HEADS_B, DK_B, DV_B), F32)],
        scratch_shapes=[pltpu.VMEM((N_HEADS_B, DV_B, DK_B), F32),
                        pltpu.VMEM((tq, kw), F32),
                        pltpu.VMEM((tq, kw), BF16),
                        pltpu.VMEM((tq, kw), BF16),
                        pltpu.VMEM((tq, kw), BF16),
                        pltpu.VMEM((cs // GLA_BLOCK, tq, kw), BF16),
                        pltpu.VMEM((nchunk, 1, kw), F32),
                        pltpu.VMEM((nchunk, N_HEADS_B, cs, cs), BF16),
                        pltpu.VMEM((tq, vw), F32),
                        pltpu.VMEM((nchunk, N_HEADS_B, DV_B, DK_B), F32)],
        compiler_params=_cparams("parallel", "arbitrary"),
        name="gla_init" if has_init else "gla",
    )(*args)


def _merge_ffn_body(x_ref, oa_ref, ob_ref, rb_ref, ga_ref, gb_ref, gob_ref, wo_ref, gf_ref,
                    wg_ref, wu_ref, wd_ref, y_ref, *, ff_chunks, row_parts):
    tm = x_ref.shape[0]
    parts = [slice(k * tm // row_parts, (k + 1) * tm // row_parts) for k in range(row_parts)]
    gob = gob_ref[...]

    def mix2(rows):
        ob = ob_ref[rows, :].astype(F32)
        obn = jnp.concatenate(
            [_rms_rows(ob[:, h * DV_B:(h + 1) * DV_B], gob) for h in range(N_HEADS_B)], axis=1)
        rbh = rb_ref[rows, :].astype(F32)
        obn = obn * (rbh * (jnp.tanh(rbh) + 1.0))
        return ((jnp.tanh(ga_ref[rows, :].astype(F32)) + 1.0) * oa_ref[rows, :].astype(F32)
                + (jnp.tanh(gb_ref[rows, :].astype(F32)) + 1.0) * obn).astype(BF16)

    mixes = [mix2(rows) for rows in parts]
    acc = [x_ref[rows, :] + jnp.dot(mx, wo_ref[...], preferred_element_type=F32)
           for rows, mx in zip(parts, mixes)]
    hs = [_rms_rows(x1, gf_ref[...]).astype(BF16) for x1 in acc]
    units = [(c0, c1, k) for c0, c1 in ff_chunks for k in range(row_parts)]
    act = {}
    for n in range(len(units) + DOWN_LAG):
        if n < len(units):
            c0, c1, k = units[n]
            gh = jnp.dot(hs[k], wg_ref[:, c0:c1], preferred_element_type=F32)
            u = jnp.dot(hs[k], wu_ref[:, c0:c1], preferred_element_type=F32)
            act[n] = ((gh * (jnp.tanh(gh) + 1.0)) * u).astype(BF16)
        if n >= DOWN_LAG:
            c0, c1, k = units[n - DOWN_LAG]
            acc[k] = acc[k] + jnp.dot(act.pop(n - DOWN_LAG), wd_ref[c0:c1, :], preferred_element_type=F32)
    for rows, y in zip(parts, acc):
        y_ref[rows, :] = y


def _ff_chunks(f, pref):
    mxu = 2 * LANES
    step = pref if f % mxu == 0 else f
    edges = list(range(0, f, step)) + [f]
    return tuple(zip(edges[:-1], edges[1:]))


def _merge_ffn(x, oa, ob, gates, gob, wo, gf, wg, wu, wd, *, tm):
    m, d = x.shape
    tile = lambda c: pl.BlockSpec((tm, d), lambda i: (i, c))
    return pl.pallas_call(
        functools.partial(_merge_ffn_body, ff_chunks=_ff_chunks(wg.shape[1], 1024),
                          row_parts=2 if tm % 32 == 0 else 1),
        grid=(m // tm,),
        in_specs=[tile(0), tile(0), tile(0), tile(0), tile(1), tile(2),
                  _resident(gob.shape), _resident(wo.shape), _resident(gf.shape),
                  _resident(wg.shape), _resident(wu.shape), _resident(wd.shape)],
        out_specs=tile(0),
        out_shape=jax.ShapeDtypeStruct((m, d), F32),
        compiler_params=_cparams("parallel"),
        name="merge_ffn",
    )(x, oa, ob, gates, gates, gates, gob, wo, gf, wg, wu, wd)


def _alibi_bias_t(cq, band, q_off, masked_prefixes):
    slopes = LOG2E * 2.0 ** (-8.0 * np.arange(1, N_HEADS_A + 1, dtype=np.float64) / N_HEADS_A)
    dist = np.abs(np.arange(band)[:, None] - q_off - np.arange(cq)[None, :])
    bias = -slopes.reshape(N_HEADS_A // 2, 1, 2, 1) * dist[None, :, None, :]
    bias = bias.reshape(N_HEADS_A // 2, band, 2 * cq)
    out = np.stack([np.where(np.arange(band)[None, :, None] < p, NEG, bias) for p in masked_prefixes])
    return jnp.asarray(out, F32)


def _sink_rows(sinks, cq):
    depth = sinks.shape[0]
    return jnp.repeat(sinks.reshape(depth, N_HEADS_A // 2, 2) * LOG2E, cq, axis=2)[:, :, None, :]


def _row_tile(m, pref):
    t = min(pref, m)
    while m % t:
        t //= 2
    return t


def kernel(x_prompt, x_sample, cache_swa_k, cache_swa_v, state_gla, w_in, w_a2, b_a, g_q, g_k, sinks,
           g_ob, w_out, g_mix, g_ffn, w_gate, w_up, w_down):
    depth = w_in.shape[0]
    bp, tp, d = x_prompt.shape
    bs, ts, _ = x_sample.shape
    past = cache_swa_k.shape[2]
    qw, kvw = N_HEADS_A * HEAD_DIM_A, N_KV_A * HEAD_DIM_A
    kw, vw = N_HEADS_B * DK_B, N_HEADS_B * DV_B
    assert past == WINDOW and tp % CHUNK == 0 and ts % GLA_BLOCK == 0 and ts <= CHUNK

    sizes = (qw, kvw, kvw, kw, kw, vw, vw, GATE_RANK, d, d)
    offs = np.concatenate([[0], np.cumsum(sizes)])
    col = lambda a, b: w_in[:, :, int(offs[a]):int(offs[b])]
    w_attn = col(0, 3).astype(BF16)
    w_gla = jnp.concatenate(
        [col(3, 6), col(7, 8), jnp.zeros((depth, d, A1_PAD - GATE_RANK), w_in.dtype)], axis=2).astype(BF16)
    w_gates = (0.5 * jnp.concatenate([col(6, 7), col(8, 10)], axis=2)).astype(BF16)
    w_a2p = jnp.concatenate(
        [w_a2, jnp.zeros((depth, A1_PAD - GATE_RANK, kw), w_a2.dtype)], axis=1).astype(BF16)
    w_out_b, w_gate_b, w_up_b, w_down_b = (w.astype(BF16) for w in (0.5 * w_out, 0.5 * w_gate, w_up, w_down))
    gq2 = jnp.tile(g_q * (HEAD_DIM_A ** -0.5 * LOG2E), (1, LANES // HEAD_DIM_A))[:, None, :]
    gk2 = jnp.tile(g_k, (1, LANES // HEAD_DIM_A))[:, None, :]
    gob4 = g_ob[:, None, :]
    bias_p = _alibi_bias_t(CHUNK, WINDOW + CHUNK, WINDOW, (0, WINDOW, WINDOW - CHUNK))
    bias_s = _alibi_bias_t(ts, past + ts, past, (0,))
    sink_p = _sink_rows(sinks, CHUNK)
    sink_s = _sink_rows(sinks, ts)

    tq_p = _row_tile(tp, 512)
    tm_p = _row_tile(bp * tp, 512)
    tm_s = _row_tile(bs * ts, 512)

    xp = x_prompt.reshape(bp * tp, d)
    xs = x_sample.reshape(bs * ts, d)
    pk, pv, ps, sk, sv, ss = [], [], [], [], [], []
    for l in range(depth):
        dense = dict(gob=gob4[l], wo=w_out_b[l], gf=g_ffn[l][None], wg=w_gate_b[l], wu=w_up_b[l], wd=w_down_b[l])
        attn, gla_in, gates = _inproj(xp, g_mix[l][None], gq2[l], gk2[l], w_attn[l], w_gla[l], w_gates[l], tm=tm_p)
        oa, kl, vl = _swa_prompt(attn, bias_p, sink_p[l], batch=bp, seq=tp, tq=tq_p)
        ob, st = _gla(gla_in, w_a2p[l], b_a[l][None], None, batch=bp, seq=tp, tq=tq_p, cs=CHUNK)
        xp = _merge_ffn(xp, oa, ob, gates, tm=tm_p, **dense)
        pk.append(kl.reshape(bp, WINDOW, N_KV_A, HEAD_DIM_A))
        pv.append(vl.reshape(bp, WINDOW, N_KV_A, HEAD_DIM_A))
        ps.append(st)
        attn, gla_in, gates = _inproj(xs, g_mix[l][None], gq2[l], gk2[l], w_attn[l], w_gla[l], w_gates[l], tm=tm_s)
        oa, kn, vn = _swa_sample(attn, cache_swa_k[l].reshape(bs, past, kvw), cache_swa_v[l].reshape(bs, past, kvw),
                                 bias_s, sink_s[l], batch=bs, tq=ts, nb=_row_tile(bs, 4))
        ob, st = _gla(gla_in, w_a2p[l], b_a[l][None], state_gla[l], batch=bs, seq=ts, tq=ts, cs=ts)
        xs = _merge_ffn(xs, oa, ob, gates, tm=tm_s, **dense)
        sk.append(kn.reshape(bs, ts, N_KV_A, HEAD_DIM_A))
        sv.append(vn.reshape(bs, ts, N_KV_A, HEAD_DIM_A))
        ss.append(st)

    return (xp.reshape(bp, tp, d), xs.reshape(bs, ts, d), jnp.stack(pk), jnp.stack(pv), jnp.stack(ps),
            jnp.stack(sk), jnp.stack(sv), jnp.stack(ss))
```

```python
import functools

import jax
import jax.numpy as jnp
import numpy as np
from jax import lax
from jax.experimental import pallas as pl
from jax.experimental.pallas import tpu as pltpu

CHUNK = 64
EPS = 1e-6
NEG = -1e30
N_HEADS_A = 16
N_KV_A = 4
GROUP_A = N_HEADS_A // N_KV_A
HEAD_DIM_A = 64
WINDOW = 128
N_HEADS_B = 4
DK_B = 128
DV_B = 256
GATE_RANK = 16
GATE_TAU = 16.0
GLA_BLOCK = 16

LANES = 128
A1_PAD = LANES
VMEM_LIMIT = 56 * 1024 * 1024
SCORE_RING = 6
PROB_RING = 4
DOWN_LAG = 2
LOG2E = 1.4426950408889634

BF16 = jnp.bfloat16
F32 = jnp.float32
NT = (((1,), (1,)), ((), ()))
TN = (((0,), (0,)), ((), ()))


def _cparams(*sem):
    return pltpu.CompilerParams(dimension_semantics=sem, vmem_limit_bytes=VMEM_LIMIT)


def _resident(shape):
    nd = len(shape)
    return pl.BlockSpec(shape, lambda *_: (0,) * nd, pipeline_mode=pl.Buffered(1))


def _layer(stacked, l):
    nd = stacked.ndim - 1
    return pl.BlockSpec((None,) + stacked.shape[1:], lambda *_: (l,) + (0,) * nd, pipeline_mode=pl.Buffered(1))


def _rms_rows(x, g):
    ms = jnp.mean(x * x, axis=-1, keepdims=True)
    return (x * lax.rsqrt(ms + EPS)) * g


def _head_norm_tile(x, g2, lo_half):
    sq = x * x
    ms_lo = jnp.sum(jnp.where(lo_half, sq, 0.0), axis=-1, keepdims=True) * (1.0 / HEAD_DIM_A)
    ms_hi = jnp.sum(jnp.where(lo_half, 0.0, sq), axis=-1, keepdims=True) * (1.0 / HEAD_DIM_A)
    r = jnp.where(lo_half, lax.rsqrt(ms_lo + EPS), lax.rsqrt(ms_hi + EPS))
    return (x * r) * g2


def _inproj_body(x_ref, g_ref, gq_ref, gk_ref, wm_ref, wg_ref, oa_ref, ob_ref, oc_ref, *, plan):
    h = _rms_rows(x_ref[...], g_ref[...]).astype(BF16)
    lo_half = lax.broadcasted_iota(jnp.int32, (1, LANES), 1) < HEAD_DIM_A
    qw, kvw = N_HEADS_A * HEAD_DIM_A, N_KV_A * HEAD_DIM_A
    w_refs, o_refs = (wm_ref, wg_ref), (oa_ref, ob_ref, oc_ref)
    for src, c0, c1, dst, d0 in plan:
        acc = jnp.dot(h, w_refs[src][:, c0:c1], preferred_element_type=F32)
        if dst != 0:
            o_refs[dst][:, d0:d0 + (c1 - c0)] = acc.astype(BF16)
            continue
        for t0 in range(0, c1 - c0, LANES):
            col = d0 + t0
            tile = acc[:, t0:t0 + LANES]
            if col < qw:
                tile = _head_norm_tile(tile, gq_ref[...], lo_half)
            elif col < qw + kvw:
                tile = _head_norm_tile(tile, gk_ref[...], lo_half)
            oa_ref[:, col:col + LANES] = tile.astype(BF16)


def _inproj_plan(d, col_chunk):
    qw, kvw = N_HEADS_A * HEAD_DIM_A, N_KV_A * HEAD_DIM_A
    kw, vw = N_HEADS_B * DK_B, N_HEADS_B * DV_B
    attn_w, gla_w = qw + 2 * kvw, 2 * kw + vw
    runs = [(0, 0, attn_w, 0, 0),
            (0, attn_w, attn_w + gla_w, 1, 0),
            (0, attn_w + gla_w + vw, attn_w + gla_w + vw + A1_PAD, 1, gla_w),
            (0, attn_w + gla_w, attn_w + gla_w + vw, 2, 0),
            (1, 0, 2 * d, 2, vw)]
    plan = []
    for src, c0, c1, dst, d0 in runs:
        for s0 in range(c0, c1, col_chunk):
            plan.append((src, s0, min(s0 + col_chunk, c1), dst, d0 + s0 - c0))
    return tuple(plan), (attn_w, gla_w + A1_PAD, vw + 2 * d)


def _inproj(x, g, gq2, gk2, w_main, w_mg, l, *, tm):
    m, d = x.shape
    plan, widths = _inproj_plan(d, 512)
    return pl.pallas_call(
        functools.partial(_inproj_body, plan=plan),
        grid=(m // tm,),
        in_specs=[pl.BlockSpec((tm, d), lambda i: (i, 0)), _resident(g.shape),
                  _resident(gq2.shape), _resident(gk2.shape), _layer(w_main, l), _layer(w_mg, l)],
        out_specs=[pl.BlockSpec((tm, w), lambda i: (i, 0)) for w in widths],
        out_shape=[jax.ShapeDtypeStruct((m, w), BF16) for w in widths],
        compiler_params=_cparams("parallel"),
        name="inproj",
    )(x, g, gq2, gk2, w_main, w_mg)


def _dup_halves(x, lo_half):
    r = pltpu.roll(x, HEAD_DIM_A, axis=1)
    return jnp.where(lo_half, x, r), jnp.where(lo_half, r, x)


def _scores(lhs, kk, bias_t):
    return lax.dot_general(kk, lhs, NT, preferred_element_type=F32) + bias_t


def _sink_softmax(s, sink_row):
    m = jnp.maximum(jnp.max(s, axis=0, keepdims=True), sink_row)
    p = jnp.exp2(s - m)
    l = jnp.sum(p, axis=0, keepdims=True) + jnp.exp2(sink_row - m)
    return (p * (1.0 / l)).astype(BF16)


def _weighted_values(p, vv, lo_half, cq):
    o = lax.dot_general(p, vv, TN, preferred_element_type=F32)
    return jnp.where(lo_half, o[0:cq], o[cq:2 * cq]).astype(BF16)


def _attend_tile(lhs, kk, vv, bias_t, sink_row, lo_half, cq):
    return _weighted_values(_sink_softmax(_scores(lhs, kk, bias_t), sink_row), vv, lo_half, cq)


def _half_masks():
    lane = lax.broadcasted_iota(jnp.int32, (1, LANES), 1)
    lo_half = lane < HEAD_DIM_A
    return lo_half, jnp.where(lo_half, 1.0, 0.0).astype(BF16), jnp.where(lo_half, 0.0, 1.0).astype(BF16)


def _swa_prompt_body(q_ref, kv_ref, bias_ref, sink_ref,
                     o_ref, klast_ref, vlast_ref, kk_s, vv_s, qm_s, s_buf, p_buf, *, tq):
    i = pl.program_id(1)
    kvw = N_KV_A * HEAD_DIM_A
    nchunk = tq // CHUNK
    lo_half, mask_lo, mask_hi = _half_masks()

    @pl.when(i == 0)
    def _():
        kk_s[:, 0:WINDOW, :] = jnp.zeros((N_KV_A, WINDOW, LANES), BF16)
        vv_s[:, 0:WINDOW, :] = jnp.zeros((N_KV_A, WINDOW, LANES), BF16)

    @pl.when(i > 0)
    def _():
        kk_s[:, 0:WINDOW, :] = kk_s[:, tq:tq + WINDOW, :]
        vv_s[:, 0:WINDOW, :] = vv_s[:, tq:tq + WINDOW, :]

    for t in range(kvw // LANES):
        k_t = kv_ref[:, t * LANES:(t + 1) * LANES].astype(F32)
        v_t = kv_ref[:, kvw + t * LANES:kvw + (t + 1) * LANES].astype(F32)
        klast_ref[0, :, t * LANES:(t + 1) * LANES] = k_t[tq - WINDOW:, :]
        vlast_ref[0, :, t * LANES:(t + 1) * LANES] = v_t[tq - WINDOW:, :]
        k0, k1 = _dup_halves(k_t, lo_half)
        v0, v1 = _dup_halves(v_t, lo_half)
        kk_s[2 * t, WINDOW:, :] = k0.astype(BF16)
        kk_s[2 * t + 1, WINDOW:, :] = k1.astype(BF16)
        vv_s[2 * t, WINDOW:, :] = v0.astype(BF16)
        vv_s[2 * t + 1, WINDOW:, :] = v1.astype(BF16)

    for t in range(N_HEADS_A * HEAD_DIM_A // LANES):
        q_t = q_ref[:, t * LANES:(t + 1) * LANES]
        qm_s[t, :, 0:CHUNK, :] = (q_t * mask_lo).reshape(nchunk, CHUNK, LANES)
        qm_s[t, :, CHUNK:, :] = (q_t * mask_hi).reshape(nchunk, CHUNK, LANES)

    band = WINDOW + CHUNK
    ntile = N_HEADS_A * HEAD_DIM_A // LANES
    nitem = nchunk * ntile
    n_sring, n_pring = s_buf.shape[0], p_buf.shape[0]
    skew_sm, skew_pv = n_sring - 2, n_pring - 1

    def keys(n):
        c, t = divmod(n, ntile)
        return t // 2, slice(c * CHUNK, c * CHUNK + band)

    def stage_scores(n):
        c, t = divmod(n, ntile)
        j, rows = keys(n)
        variant = jnp.where(i == 0, c + 1, 0) if c < WINDOW // CHUNK else 0
        s_buf[n % n_sring] = _scores(qm_s[t, c], kk_s[j, rows, :], bias_ref[variant, t])

    def stage_softmax(n):
        p_buf[n % n_pring] = _sink_softmax(s_buf[n % n_sring], sink_ref[n % ntile])

    def stage_values(n):
        c, t = divmod(n, ntile)
        j, rows = keys(n)
        o_ref[c * CHUNK:(c + 1) * CHUNK, t * LANES:(t + 1) * LANES] = _weighted_values(
            p_buf[n % n_pring], vv_s[j, rows, :], lo_half, CHUNK)

    for n in range(nitem + skew_sm + skew_pv):
        if n < nitem:
            stage_scores(n)
        if 0 <= n - skew_sm < nitem:
            stage_softmax(n - skew_sm)
        if 0 <= n - skew_sm - skew_pv < nitem:
            stage_values(n - skew_sm - skew_pv)


def _swa_prompt(attn, bias, sink_rows, *, batch, seq, tq):
    m = attn.shape[0]
    nt = seq // tq
    qw = N_HEADS_A * HEAD_DIM_A
    kvw = N_KV_A * HEAD_DIM_A
    row = lambda b, i: b * nt + i
    return pl.pallas_call(
        functools.partial(_swa_prompt_body, tq=tq),
        grid=(batch, nt),
        in_specs=[pl.BlockSpec((tq, qw), lambda b, i: (row(b, i), 0)),
                  pl.BlockSpec((tq, 2 * kvw), lambda b, i: (row(b, i), qw // (2 * kvw))),
                  _resident(bias.shape), _resident(sink_rows.shape)],
        out_specs=[pl.BlockSpec((tq, qw), lambda b, i: (row(b, i), 0)),
                   pl.BlockSpec((1, WINDOW, kvw), lambda b, i: (b, 0, 0)),
                   pl.BlockSpec((1, WINDOW, kvw), lambda b, i: (b, 0, 0))],
        out_shape=[jax.ShapeDtypeStruct((m, qw), BF16),
                   jax.ShapeDtypeStruct((batch, WINDOW, kvw), F32),
                   jax.ShapeDtypeStruct((batch, WINDOW, kvw), F32)],
        scratch_shapes=[pltpu.VMEM((N_KV_A, WINDOW + tq, LANES), BF16),
                        pltpu.VMEM((N_KV_A, WINDOW + tq, LANES), BF16),
                        pltpu.VMEM((qw // LANES, tq // CHUNK, 2 * CHUNK, LANES), BF16),
                        pltpu.VMEM((SCORE_RING, WINDOW + CHUNK, LANES), F32),
                        pltpu.VMEM((PROB_RING, WINDOW + CHUNK, LANES), BF16)],
        compiler_params=_cparams("parallel", "arbitrary"),
        name="swa_prompt",
    )(attn, attn, bias, sink_rows)


def _swa_sample_body(q_ref, kv_ref, ck_ref, cv_ref, bias_ref, sink_ref,
                     o_ref, knew_ref, vnew_ref, *, tq, nb):
    kvw = N_KV_A * HEAD_DIM_A
    lo_half, mask_lo, mask_hi = _half_masks()
    for b in range(nb):
        rows = slice(b * tq, (b + 1) * tq)
        kk, vv = [None] * N_KV_A, [None] * N_KV_A
        for t in range(kvw // LANES):
            sl = slice(t * LANES, (t + 1) * LANES)
            k_t = kv_ref[rows, sl].astype(F32)
            v_t = kv_ref[rows, kvw + t * LANES:kvw + (t + 1) * LANES].astype(F32)
            knew_ref[b, :, sl] = k_t
            vnew_ref[b, :, sl] = v_t
            k0, k1 = _dup_halves(jnp.concatenate([ck_ref[b, :, sl], k_t], axis=0), lo_half)
            v0, v1 = _dup_halves(jnp.concatenate([cv_ref[b, :, sl], v_t], axis=0), lo_half)
            kk[2 * t], kk[2 * t + 1] = k0.astype(BF16), k1.astype(BF16)
            vv[2 * t], vv[2 * t + 1] = v0.astype(BF16), v1.astype(BF16)
        for t in range(N_HEADS_A * HEAD_DIM_A // LANES):
            q_t = q_ref[rows, t * LANES:(t + 1) * LANES]
            lhs = jnp.concatenate([q_t * mask_lo, q_t * mask_hi], axis=0)
            o_ref[rows, t * LANES:(t + 1) * LANES] = _attend_tile(
                lhs, kk[t // 2], vv[t // 2], bias_ref[0, t], sink_ref[t], lo_half, tq)


def _swa_sample(attn, cache_k, cache_v, l, bias, sink_rows, *, batch, tq, nb):
    m = attn.shape[0]
    qw = N_HEADS_A * HEAD_DIM_A
    kvw = N_KV_A * HEAD_DIM_A
    past = cache_k.shape[2]
    return pl.pallas_call(
        functools.partial(_swa_sample_body, tq=tq, nb=nb),
        grid=(batch // nb,),
        in_specs=[pl.BlockSpec((nb * tq, qw), lambda b: (b, 0)),
                  pl.BlockSpec((nb * tq, 2 * kvw), lambda b: (b, qw // (2 * kvw))),
                  pl.BlockSpec((None, nb, past, kvw), lambda b: (l, b, 0, 0)),
                  pl.BlockSpec((None, nb, past, kvw), lambda b: (l, b, 0, 0)),
                  _resident(bias.shape), _resident(sink_rows.shape)],
        out_specs=[pl.BlockSpec((nb * tq, qw), lambda b: (b, 0)),
                   pl.BlockSpec((nb, tq, kvw), lambda b: (b, 0, 0)),
                   pl.BlockSpec((nb, tq, kvw), lambda b: (b, 0, 0))],
        out_shape=[jax.ShapeDtypeStruct((m, qw), BF16),
                   jax.ShapeDtypeStruct((batch, tq, kvw), F32),
                   jax.ShapeDtypeStruct((batch, tq, kvw), F32)],
        compiler_params=_cparams("parallel"),
        name="swa_sample",
    )(attn, attn, cache_k, cache_v, bias, sink_rows)


def _log_sigmoid(z):
    return -(jnp.maximum(-z, 0.0) + jnp.log(1.0 + jnp.exp(-jnp.abs(z))))


def _gla_body(*refs, tq, cs, has_init):
    if has_init:
        (qk_ref, v_ref, a1_ref, wa2_ref, ba_ref, s0_ref, o_ref, sout_ref,
         st_s, cum_s, qin_s, kout_s, qs_s, ks_s, dec_s, att_s, oi_s, upd_s) = refs
    else:
        (qk_ref, v_ref, a1_ref, wa2_ref, ba_ref, o_ref, sout_ref,
         st_s, cum_s, qin_s, kout_s, qs_s, ks_s, dec_s, att_s, oi_s, upd_s) = refs
    i = pl.program_id(1)
    kw = N_HEADS_B * DK_B
    nchunk = tq // cs
    nsb = cs // GLA_BLOCK
    scale = DK_B ** -0.5
    chunk_rows = [slice(c * cs, (c + 1) * cs) for c in range(nchunk)]
    ksls = [slice(h * DK_B, (h + 1) * DK_B) for h in range(N_HEADS_B)]
    vsls = [slice(h * DV_B, (h + 1) * DV_B) for h in range(N_HEADS_B)]

    @pl.when(i == 0)
    def _():
        for h in range(N_HEADS_B):
            if has_init:
                st_s[h] = s0_ref[0, h].T
            else:
                st_s[h] = jnp.zeros((DV_B, DK_B), F32)

    z = jnp.dot(a1_ref[...], wa2_ref[...], preferred_element_type=F32) + ba_ref[...]
    la = _log_sigmoid(z) * (1.0 / GATE_TAU)
    row = lax.broadcasted_iota(jnp.int32, (cs, cs), 0)
    col = lax.broadcasted_iota(jnp.int32, (cs, cs), 1)
    causal = col <= row
    tri = jnp.where(causal, 1.0, 0.0).astype(BF16)
    la_hi = la.astype(BF16)
    la_lo = (la - la_hi.astype(F32)).astype(BF16)

    def stage1(c):
        rows = chunk_rows[c]
        cum_s[rows, :] = (jnp.dot(tri, la_hi[rows], preferred_element_type=F32)
                          + jnp.dot(tri, la_lo[rows], preferred_element_type=F32))

    def stage2(c):
        rows = chunk_rows[c]
        cum = cum_s[rows, :]
        q = qk_ref[rows, 0:kw].astype(F32)
        k = qk_ref[rows, kw:2 * kw].astype(F32)
        last = cum[cs - 1:cs, :]
        qin_s[rows, :] = (q * (jnp.exp(cum) * scale)).astype(BF16)
        kout_s[rows, :] = (k * jnp.exp(last - cum)).astype(BF16)
        dec_s[c] = jnp.exp(last)
        for sb in range(nsb):
            lo, hi = sb * GLA_BLOCK, (sb + 1) * GLA_BLOCK
            ref_row = cum[lo - 1:lo, :] if sb > 0 else jnp.zeros_like(last)
            qs_s[c * cs + lo:c * cs + hi, :] = (q[lo:hi] * (jnp.exp(cum[lo:hi] - ref_row) * scale)).astype(BF16)
            ks_s[sb, c * cs:c * cs + hi, :] = (k[0:hi] * jnp.exp(ref_row - cum[0:hi])).astype(BF16)
            if hi < cs:
                ks_s[sb, c * cs + hi:(c + 1) * cs, :] = jnp.zeros((cs - hi, kw), BF16)

    def stage3a(c):
        rows = chunk_rows[c]
        for h in range(N_HEADS_B):
            att = jnp.concatenate(
                [lax.dot_general(qs_s[c * cs + sb * GLA_BLOCK:c * cs + (sb + 1) * GLA_BLOCK, ksls[h]],
                                 ks_s[sb, rows, ksls[h]], NT, preferred_element_type=F32) for sb in range(nsb)],
                axis=0)
            att_s[c, h] = jnp.where(causal, att, 0.0).astype(BF16)

    def stage3b(c):
        rows = chunk_rows[c]
        for h in range(N_HEADS_B):
            oi_s[rows, vsls[h]] = jnp.dot(att_s[c, h], v_ref[rows, vsls[h]], preferred_element_type=F32)
            upd_s[c, h] = lax.dot_general(v_ref[rows, vsls[h]], kout_s[rows, ksls[h]], TN,
                                          preferred_element_type=F32)

    def stage4(c):
        rows = chunk_rows[c]
        dec = dec_s[c]
        for h in range(N_HEADS_B):
            st = st_s[h]
            o = oi_s[rows, vsls[h]] + lax.dot_general(qin_s[rows, ksls[h]], st.astype(BF16), NT,
                                                      preferred_element_type=F32)
            o_ref[rows, vsls[h]] = o.astype(BF16)
            st_s[h] = st * dec[:, ksls[h]] + upd_s[c, h]

    stages = (stage1, stage2, stage3a, stage3b, stage4)
    for n in range(nchunk + len(stages) - 1):
        for depth, stage in enumerate(stages):
            if 0 <= n - depth < nchunk:
                stage(n - depth)

    @pl.when(i == pl.num_programs(1) - 1)
    def _():
        for h in range(N_HEADS_B):
            sout_ref[0, h] = st_s[h].T


def _gla(gla_in, wa2p, ba, s0, l, *, batch, seq, tq, cs):
    m = gla_in.shape[0]
    nt = seq // tq
    kw = N_HEADS_B * DK_B
    vw = N_HEADS_B * DV_B
    row = lambda b, i: b * nt + i
    has_init = s0 is not None
    in_specs = [pl.BlockSpec((tq, 2 * kw), lambda b, i: (row(b, i), 0)),
                pl.BlockSpec((tq, vw), lambda b, i: (row(b, i), 2 * kw // vw)),
                pl.BlockSpec((tq, A1_PAD), lambda b, i: (row(b, i), (2 * kw + vw) // A1_PAD)),
                _layer(wa2p, l), _resident(ba.shape)]
    args = [gla_in, gla_in, gla_in, wa2p, ba]
    state_spec = pl.BlockSpec((1, N_HEADS_B, DK_B, DV_B), lambda b, i: (b, 0, 0, 0))
    if has_init:
        in_specs.append(pl.BlockSpec((None, 1, N_HEADS_B, DK_B, DV_B), lambda b, i: (l, b, 0, 0, 0)))
        args.append(s0)
    nchunk = tq // cs
    return pl.pallas_call(
        functools.partial(_gla_body, tq=tq, cs=cs, has_init=has_init),
        grid=(batch, nt),
        in_specs=in_specs,
        out_specs=[pl.BlockSpec((tq, vw), lambda b, i: (row(b, i), 0)), state_spec],
        out_shape=[jax.ShapeDtypeStruct((m, vw), BF16),
                   jax.ShapeDtypeStruct((batch, N_HEADS_B, DK_B, DV_B), F32)],
        scratch_shapes=[pltpu.VMEM((N_HEADS_B, DV_B, DK_B), F32),
                        pltpu.VMEM((tq, kw), F32),
                        pltpu.VMEM((tq, kw), BF16),
                        pltpu.VMEM((tq, kw), BF16),
                        pltpu.VMEM((tq, kw), BF16),
                        pltpu.VMEM((cs // GLA_BLOCK, tq, kw), BF16),
                        pltpu.VMEM((nchunk, 1, kw), F32),
                        pltpu.VMEM((nchunk, N_HEADS_B, cs, cs), BF16),
                        pltpu.VMEM((tq, vw), F32),
                        pltpu.VMEM((nchunk, N_HEADS_B, DV_B, DK_B), F32)],
        compiler_params=_cparams("parallel", "arbitrary"),
        name="gla_init" if has_init else "gla",
    )(*args)


def _merge_ffn_body(x_ref, oa_ref, ob_ref, rb_ref, ga_ref, gb_ref, gob_ref, wo_ref, gf_ref,
                    wg_ref, wu_ref, wd_ref, y_ref, *, ff_chunks, row_parts):
    tm = x_ref.shape[0]
    parts = [slice(k * tm // row_parts, (k + 1) * tm // row_parts) for k in range(row_parts)]
    gob = gob_ref[...]

    def mix2(rows):
        ob = ob_ref[rows, :].astype(F32)
        obn = jnp.concatenate(
            [_rms_rows(ob[:, h * DV_B:(h + 1) * DV_B], gob) for h in range(N_HEADS_B)], axis=1)
        rbh = rb_ref[rows, :].astype(F32)
        obn = obn * (rbh * (jnp.tanh(rbh) + 1.0))
        return ((jnp.tanh(ga_ref[rows, :].astype(F32)) + 1.0) * oa_ref[rows, :].astype(F32)
                + (jnp.tanh(gb_ref[rows, :].astype(F32)) + 1.0) * obn).astype(BF16)

    mixes = [mix2(rows) for rows in parts]
    acc = [x_ref[rows, :] + jnp.dot(mx, wo_ref[...], preferred_element_type=F32)
           for rows, mx in zip(parts, mixes)]
    hs = [_rms_rows(x1, gf_ref[...]).astype(BF16) for x1 in acc]
    units = [(c0, c1, k) for c0, c1 in ff_chunks for k in range(row_parts)]
    act = {}
    for n in range(len(units) + DOWN_LAG):
        if n < len(units):
            c0, c1, k = units[n]
            gh = jnp.dot(hs[k], wg_ref[:, c0:c1], preferred_element_type=F32)
            u = jnp.dot(hs[k], wu_ref[:, c0:c1], preferred_element_type=F32)
            act[n] = ((gh * (jnp.tanh(gh) + 1.0)) * u).astype(BF16)
        if n >= DOWN_LAG:
            c0, c1, k = units[n - DOWN_LAG]
            acc[k] = acc[k] + jnp.dot(act.pop(n - DOWN_LAG), wd_ref[c0:c1, :], preferred_element_type=F32)
    for rows, y in zip(parts, acc):
        y_ref[rows, :] = y


def _ff_chunks(f, pref):
    mxu = 2 * LANES
    step = pref if f % mxu == 0 else f
    edges = list(range(0, f, step)) + [f]
    return tuple(zip(edges[:-1], edges[1:]))


def _merge_ffn(x, oa, ob, gates, gob, wo, gf, wg, wu, wd, l, *, tm):
    m, d = x.shape
    tile = lambda c: pl.BlockSpec((tm, d), lambda i: (i, c))
    return pl.pallas_call(
        functools.partial(_merge_ffn_body, ff_chunks=_ff_chunks(wg.shape[2], 1024),
                          row_parts=2 if tm % 32 == 0 else 1),
        grid=(m // tm,),
        in_specs=[tile(0), tile(0), tile(0), tile(0), tile(1), tile(2),
                  _resident(gob.shape), _layer(wo, l), _resident(gf.shape),
                  _layer(wg, l), _layer(wu, l), _layer(wd, l)],
        out_specs=tile(0),
        out_shape=jax.ShapeDtypeStruct((m, d), F32),
        compiler_params=_cparams("parallel"),
        name="merge_ffn",
    )(x, oa, ob, gates, gates, gates, gob, wo, gf, wg, wu, wd)


def _alibi_bias_t(cq, band, q_off, masked_prefixes):
    slopes = LOG2E * 2.0 ** (-8.0 * np.arange(1, N_HEADS_A + 1, dtype=np.float64) / N_HEADS_A)
    dist = np.abs(np.arange(band)[:, None] - q_off - np.arange(cq)[None, :])
    bias = -slopes.reshape(N_HEADS_A // 2, 1, 2, 1) * dist[None, :, None, :]
    bias = bias.reshape(N_HEADS_A // 2, band, 2 * cq)
    out = np.stack([np.where(np.arange(band)[None, :, None] < p, NEG, bias) for p in masked_prefixes])
    return jnp.asarray(out, F32)


def _sink_rows(sinks, cq):
    depth = sinks.shape[0]
    return jnp.repeat(sinks.reshape(depth, N_HEADS_A // 2, 2) * LOG2E, cq, axis=2)[:, :, None, :]


def _row_tile(m, pref):
    t = min(pref, m)
    while m % t:
        t //= 2
    return t


def kernel(x_prompt, x_sample, cache_swa_k, cache_swa_v, state_gla, w_in, w_a2, b_a, g_q, g_k, sinks,
           g_ob, w_out, g_mix, g_ffn, w_gate, w_up, w_down):
    depth = w_in.shape[0]
    bp, tp, d = x_prompt.shape
    bs, ts, _ = x_sample.shape
    past = cache_swa_k.shape[2]
    qw, kvw = N_HEADS_A * HEAD_DIM_A, N_KV_A * HEAD_DIM_A
    kw, vw = N_HEADS_B * DK_B, N_HEADS_B * DV_B
    assert past == WINDOW and tp % CHUNK == 0 and ts % GLA_BLOCK == 0 and ts <= CHUNK

    rb0 = qw + 2 * kvw + 2 * kw + vw
    a10 = rb0 + vw
    col_scale = np.ones((a10 + A1_PAD,), np.float32)
    col_scale[rb0:a10] = 0.5
    w_main = (w_in[:, :, :a10 + A1_PAD] * col_scale).astype(BF16)
    w_mg = (0.5 * w_in[:, :, a10 + GATE_RANK:]).astype(BF16)
    w_a2p = jnp.concatenate(
        [w_a2, jnp.zeros((depth, A1_PAD - GATE_RANK, kw), w_a2.dtype)], axis=1).astype(BF16)
    w_out_b, w_gate_b, w_up_b, w_down_b = (w.astype(BF16) for w in (0.5 * w_out, 0.5 * w_gate, w_up, w_down))
    gq2 = jnp.tile(g_q * (HEAD_DIM_A ** -0.5 * LOG2E), (1, LANES // HEAD_DIM_A))[:, None, :]
    gk2 = jnp.tile(g_k, (1, LANES // HEAD_DIM_A))[:, None, :]
    gob4 = g_ob[:, None, :]
    bias_p = _alibi_bias_t(CHUNK, WINDOW + CHUNK, WINDOW, (0, WINDOW, WINDOW - CHUNK))
    bias_s = _alibi_bias_t(ts, past + ts, past, (0,))
    sink_p = _sink_rows(sinks, CHUNK)
    sink_s = _sink_rows(sinks, ts)

    tq_p = _row_tile(tp, 512)
    tm_p = _row_tile(bp * tp, 512)
    tm_s = _row_tile(bs * ts, 512)

    xp = x_prompt.reshape(bp * tp, d)
    xs = x_sample.reshape(bs * ts, d)
    pk, pv, ps, sk, sv, ss = [], [], [], [], [], []
    cache_k = cache_swa_k.reshape(depth, bs, past, kvw)
    cache_v = cache_swa_v.reshape(depth, bs, past, kvw)
    for l in range(depth):
        dense = dict(gob=gob4[l], wo=w_out_b, gf=g_ffn[l][None], wg=w_gate_b, wu=w_up_b, wd=w_down_b, l=l)
        attn, gla_in, gates = _inproj(xp, g_mix[l][None], gq2[l], gk2[l], w_main, w_mg, l, tm=tm_p)
        oa, kl, vl = _swa_prompt(attn, bias_p, sink_p[l], batch=bp, seq=tp, tq=tq_p)
        ob, st = _gla(gla_in, w_a2p, b_a[l][None], None, l, batch=bp, seq=tp, tq=tq_p, cs=CHUNK)
        xp = _merge_ffn(xp, oa, ob, gates, tm=tm_p, **dense)
        pk.append(kl.reshape(bp, WINDOW, N_KV_A, HEAD_DIM_A))
        pv.append(vl.reshape(bp, WINDOW, N_KV_A, HEAD_DIM_A))
        ps.append(st)
        attn, gla_in, gates = _inproj(xs, g_mix[l][None], gq2[l], gk2[l], w_main, w_mg, l, tm=tm_s)
        oa, kn, vn = _swa_sample(attn, cache_k, cache_v, l, bias_s, sink_s[l], batch=bs, tq=ts, nb=_row_tile(bs, 4))
        ob, st = _gla(gla_in, w_a2p, b_a[l][None], state_gla, l, batch=bs, seq=ts, tq=ts, cs=ts)
        xs = _merge_ffn(xs, oa, ob, gates, tm=tm_s, **dense)
        sk.append(kn.reshape(bs, ts, N_KV_A, HEAD_DIM_A))
        sv.append(vn.reshape(bs, ts, N_KV_A, HEAD_DIM_A))
        ss.append(st)

    return (xp.reshape(bp, tp, d), xs.reshape(bs, ts, d), jnp.stack(pk), jnp.stack(pv), jnp.stack(ps),
            jnp.stack(sk), jnp.stack(sv), jnp.stack(ss))
```

```python
import functools

import jax
import jax.numpy as jnp
import numpy as np
from jax import lax
from jax.experimental import pallas as pl
from jax.experimental.pallas import tpu as pltpu

CHUNK = 64
EPS = 1e-6
NEG = -1e30
N_HEADS_A = 16
N_KV_A = 4
GROUP_A = N_HEADS_A // N_KV_A
HEAD_DIM_A = 64
WINDOW = 128
N_HEADS_B = 4
DK_B = 128
DV_B = 256
GATE_RANK = 16
GATE_TAU = 16.0
GLA_BLOCK = 16

LANES = 128
A1_PAD = LANES
VMEM_LIMIT = 56 * 1024 * 1024
SCORE_RING = 6
PROB_RING = 4
PAD_ROWS = 16
DOWN_LAG = 2
LOG2E = 1.4426950408889634

BF16 = jnp.bfloat16
F32 = jnp.float32
NT = (((1,), (1,)), ((), ()))
TN = (((0,), (0,)), ((), ()))


def _cparams(*sem):
    return pltpu.CompilerParams(dimension_semantics=sem, vmem_limit_bytes=VMEM_LIMIT)


def _resident(shape):
    nd = len(shape)
    return pl.BlockSpec(shape, lambda *_: (0,) * nd, pipeline_mode=pl.Buffered(1))


def _layer(stacked, l):
    nd = stacked.ndim - 1
    return pl.BlockSpec((None,) + stacked.shape[1:], lambda *_: (l,) + (0,) * nd, pipeline_mode=pl.Buffered(1))


def _rms_rows(x, g):
    ms = jnp.mean(x * x, axis=-1, keepdims=True)
    return (x * lax.rsqrt(ms + EPS)) * g


def _head_norm_tile(x, g2, lo_half):
    sq = x * x
    ms_lo = jnp.sum(jnp.where(lo_half, sq, 0.0), axis=-1, keepdims=True) * (1.0 / HEAD_DIM_A)
    ms_hi = jnp.sum(jnp.where(lo_half, 0.0, sq), axis=-1, keepdims=True) * (1.0 / HEAD_DIM_A)
    r = jnp.where(lo_half, lax.rsqrt(ms_lo + EPS), lax.rsqrt(ms_hi + EPS))
    return (x * r) * g2


def _inproj_body(x_ref, g_ref, gq_ref, gk_ref, wm_ref, wg_ref, oa_ref, ob_ref, oc_ref, *, plan):
    h = _rms_rows(x_ref[...], g_ref[...]).astype(BF16)
    lo_half = lax.broadcasted_iota(jnp.int32, (1, LANES), 1) < HEAD_DIM_A
    qw, kvw = N_HEADS_A * HEAD_DIM_A, N_KV_A * HEAD_DIM_A
    w_refs, o_refs = (wm_ref, wg_ref), (oa_ref, ob_ref, oc_ref)
    for src, c0, c1, dst, d0 in plan:
        acc = jnp.dot(h, w_refs[src][:, c0:c1], preferred_element_type=F32)
        if dst != 0:
            o_refs[dst][:, d0:d0 + (c1 - c0)] = acc.astype(BF16)
            continue
        for t0 in range(0, c1 - c0, LANES):
            col = d0 + t0
            tile = acc[:, t0:t0 + LANES]
            if col < qw:
                tile = _head_norm_tile(tile, gq_ref[...], lo_half)
            elif col < qw + kvw:
                tile = _head_norm_tile(tile, gk_ref[...], lo_half)
            oa_ref[:, col:col + LANES] = tile.astype(BF16)


def _inproj_plan(d, col_chunk):
    qw, kvw = N_HEADS_A * HEAD_DIM_A, N_KV_A * HEAD_DIM_A
    kw, vw = N_HEADS_B * DK_B, N_HEADS_B * DV_B
    attn_w, gla_w = qw + 2 * kvw, 2 * kw + vw
    runs = [(0, 0, attn_w, 0, 0),
            (0, attn_w, attn_w + gla_w, 1, 0),
            (0, attn_w + gla_w + vw, attn_w + gla_w + vw + A1_PAD, 1, gla_w),
            (0, attn_w + gla_w, attn_w + gla_w + vw, 2, 0),
            (1, 0, 2 * d, 2, vw)]
    plan = []
    for src, c0, c1, dst, d0 in runs:
        for s0 in range(c0, c1, col_chunk):
            plan.append((src, s0, min(s0 + col_chunk, c1), dst, d0 + s0 - c0))
    return tuple(plan), (attn_w, gla_w + A1_PAD, vw + 2 * d)


def _inproj(x, g, gq2, gk2, w_main, w_mg, l, *, tm):
    m, d = x.shape
    plan, widths = _inproj_plan(d, 512)
    return pl.pallas_call(
        functools.partial(_inproj_body, plan=plan),
        grid=(m // tm,),
        in_specs=[pl.BlockSpec((tm, d), lambda i: (i, 0)), _resident(g.shape),
                  _resident(gq2.shape), _resident(gk2.shape), _layer(w_main, l), _layer(w_mg, l)],
        out_specs=[pl.BlockSpec((tm, w), lambda i: (i, 0)) for w in widths],
        out_shape=[jax.ShapeDtypeStruct((m, w), BF16) for w in widths],
        compiler_params=_cparams("parallel"),
        name="inproj",
    )(x, g, gq2, gk2, w_main, w_mg)


def _dup_halves(x, lo_half):
    r = pltpu.roll(x, HEAD_DIM_A, axis=1)
    return jnp.where(lo_half, x, r), jnp.where(lo_half, r, x)


def _scores(lhs, kk, bias_t):
    return lax.dot_general(kk, lhs, NT, preferred_element_type=F32) + bias_t


def _sink_weights(s, sink_row):
    m = jnp.maximum(jnp.max(s, axis=0, keepdims=True), sink_row)
    p = jnp.exp2(s - m).astype(BF16)
    first = lax.broadcasted_iota(jnp.int32, (PAD_ROWS, s.shape[1]), 0) == 0
    tail = jnp.where(first, jnp.exp2(sink_row - m), 0.0).astype(BF16)
    return jnp.concatenate([p, tail], axis=0)


def _value_operand_tail(band):
    rows = band + PAD_ROWS
    counted = lax.broadcasted_iota(jnp.int32, (rows, LANES), 0) <= band
    return jnp.zeros((PAD_ROWS, LANES), BF16), jnp.where(counted, 1.0, 0.0).astype(BF16)


def _weighted_values(p, vv, tail, lo_half, cq):
    zero_rows, ones_blk = tail
    rhs = jnp.concatenate([jnp.concatenate([vv, zero_rows], axis=0), ones_blk], axis=1)
    o = lax.dot_general(p, rhs, TN, preferred_element_type=F32)
    num = jnp.where(lo_half, o[0:cq, 0:LANES], o[cq:2 * cq, 0:LANES])
    den = jnp.where(lo_half, o[0:cq, LANES:], o[cq:2 * cq, LANES:])
    return (num / den).astype(BF16)


def _half_masks():
    lane = lax.broadcasted_iota(jnp.int32, (1, LANES), 1)
    lo_half = lane < HEAD_DIM_A
    return lo_half, jnp.where(lo_half, 1.0, 0.0).astype(BF16), jnp.where(lo_half, 0.0, 1.0).astype(BF16)


def _swa_prompt_body(q_ref, kv_ref, bias_ref, sink_ref,
                     o_ref, klast_ref, vlast_ref, kk_s, vv_s, qm_s, s_buf, p_buf, *, tq):
    i = pl.program_id(1)
    kvw = N_KV_A * HEAD_DIM_A
    nchunk = tq // CHUNK
    lo_half, mask_lo, mask_hi = _half_masks()

    @pl.when(i == 0)
    def _():
        kk_s[:, 0:WINDOW, :] = jnp.zeros((N_KV_A, WINDOW, LANES), BF16)
        vv_s[:, 0:WINDOW, :] = jnp.zeros((N_KV_A, WINDOW, LANES), BF16)

    @pl.when(i > 0)
    def _():
        kk_s[:, 0:WINDOW, :] = kk_s[:, tq:tq + WINDOW, :]
        vv_s[:, 0:WINDOW, :] = vv_s[:, tq:tq + WINDOW, :]

    for t in range(kvw // LANES):
        k_t = kv_ref[:, t * LANES:(t + 1) * LANES].astype(F32)
        v_t = kv_ref[:, kvw + t * LANES:kvw + (t + 1) * LANES].astype(F32)
        klast_ref[0, :, t * LANES:(t + 1) * LANES] = k_t[tq - WINDOW:, :]
        vlast_ref[0, :, t * LANES:(t + 1) * LANES] = v_t[tq - WINDOW:, :]
        k0, k1 = _dup_halves(k_t, lo_half)
        v0, v1 = _dup_halves(v_t, lo_half)
        kk_s[2 * t, WINDOW:, :] = k0.astype(BF16)
        kk_s[2 * t + 1, WINDOW:, :] = k1.astype(BF16)
        vv_s[2 * t, WINDOW:, :] = v0.astype(BF16)
        vv_s[2 * t + 1, WINDOW:, :] = v1.astype(BF16)

    for t in range(N_HEADS_A * HEAD_DIM_A // LANES):
        q_t = q_ref[:, t * LANES:(t + 1) * LANES]
        qm_s[t, :, 0:CHUNK, :] = (q_t * mask_lo).reshape(nchunk, CHUNK, LANES)
        qm_s[t, :, CHUNK:, :] = (q_t * mask_hi).reshape(nchunk, CHUNK, LANES)

    band = WINDOW + CHUNK
    tail = _value_operand_tail(band)
    ntile = N_HEADS_A * HEAD_DIM_A // LANES
    nitem = nchunk * ntile
    n_sring, n_pring = s_buf.shape[0], p_buf.shape[0]
    skew_sm, skew_pv = n_sring - 2, n_pring - 1

    def keys(n):
        c, t = divmod(n, ntile)
        return t // 2, slice(c * CHUNK, c * CHUNK + band)

    def stage_scores(n):
        c, t = divmod(n, ntile)
        j, rows = keys(n)
        variant = jnp.where(i == 0, c + 1, 0) if c < WINDOW // CHUNK else 0
        s_buf[n % n_sring] = _scores(qm_s[t, c], kk_s[j, rows, :], bias_ref[variant, t])

    def stage_softmax(n):
        p_buf[n % n_pring] = _sink_weights(s_buf[n % n_sring], sink_ref[n % ntile])

    def stage_values(n):
        c, t = divmod(n, ntile)
        j, rows = keys(n)
        o_ref[c * CHUNK:(c + 1) * CHUNK, t * LANES:(t + 1) * LANES] = _weighted_values(
            p_buf[n % n_pring], vv_s[j, rows, :], tail, lo_half, CHUNK)

    for n in range(nitem + skew_sm + skew_pv):
        if n < nitem:
            stage_scores(n)
        if 0 <= n - skew_sm < nitem:
            stage_softmax(n - skew_sm)
        if 0 <= n - skew_sm - skew_pv < nitem:
            stage_values(n - skew_sm - skew_pv)


def _swa_prompt(attn, bias, sink_rows, *, batch, seq, tq):
    m = attn.shape[0]
    nt = seq // tq
    qw = N_HEADS_A * HEAD_DIM_A
    kvw = N_KV_A * HEAD_DIM_A
    row = lambda b, i: b * nt + i
    return pl.pallas_call(
        functools.partial(_swa_prompt_body, tq=tq),
        grid=(batch, nt),
        in_specs=[pl.BlockSpec((tq, qw), lambda b, i: (row(b, i), 0)),
                  pl.BlockSpec((tq, 2 * kvw), lambda b, i: (row(b, i), qw // (2 * kvw))),
                  _resident(bias.shape), _resident(sink_rows.shape)],
        out_specs=[pl.BlockSpec((tq, qw), lambda b, i: (row(b, i), 0)),
                   pl.BlockSpec((1, WINDOW, kvw), lambda b, i: (b, 0, 0)),
                   pl.BlockSpec((1, WINDOW, kvw), lambda b, i: (b, 0, 0))],
        out_shape=[jax.ShapeDtypeStruct((m, qw), BF16),
                   jax.ShapeDtypeStruct((batch, WINDOW, kvw), F32),
                   jax.ShapeDtypeStruct((batch, WINDOW, kvw), F32)],
        scratch_shapes=[pltpu.VMEM((N_KV_A, WINDOW + tq, LANES), BF16),
                        pltpu.VMEM((N_KV_A, WINDOW + tq, LANES), BF16),
                        pltpu.VMEM((qw // LANES, tq // CHUNK, 2 * CHUNK, LANES), BF16),
                        pltpu.VMEM((SCORE_RING, WINDOW + CHUNK, LANES), F32),
                        pltpu.VMEM((PROB_RING, WINDOW + CHUNK + PAD_ROWS, LANES), BF16)],
        compiler_params=_cparams("parallel", "arbitrary"),
        name="swa_prompt",
    )(attn, attn, bias, sink_rows)


def _swa_sample_body(q_ref, kv_ref, ck_ref, cv_ref, bias_ref, sink_ref,
                     o_ref, knew_ref, vnew_ref, *, tq, nb):
    kvw = N_KV_A * HEAD_DIM_A
    lo_half, mask_lo, mask_hi = _half_masks()
    tail = _value_operand_tail(ck_ref.shape[1] + tq)
    ntile = N_HEADS_A * HEAD_DIM_A // LANES
    vvs, scores = [], []
    for b in range(nb):
        rows = slice(b * tq, (b + 1) * tq)
        kk, vv = [None] * N_KV_A, [None] * N_KV_A
        for t in range(kvw // LANES):
            sl = slice(t * LANES, (t + 1) * LANES)
            k_t = kv_ref[rows, sl].astype(F32)
            v_t = kv_ref[rows, kvw + t * LANES:kvw + (t + 1) * LANES].astype(F32)
            knew_ref[b, :, sl] = k_t
            vnew_ref[b, :, sl] = v_t
            k0, k1 = _dup_halves(jnp.concatenate([ck_ref[b, :, sl], k_t], axis=0), lo_half)
            v0, v1 = _dup_halves(jnp.concatenate([cv_ref[b, :, sl], v_t], axis=0), lo_half)
            kk[2 * t], kk[2 * t + 1] = k0.astype(BF16), k1.astype(BF16)
            vv[2 * t], vv[2 * t + 1] = v0.astype(BF16), v1.astype(BF16)
        vvs.append(vv)
        for t in range(ntile):
            q_t = q_ref[rows, t * LANES:(t + 1) * LANES]
            lhs = jnp.concatenate([q_t * mask_lo, q_t * mask_hi], axis=0)
            scores.append(_scores(lhs, kk[t // 2], bias_ref[0, t]))
    weights = [_sink_weights(s, sink_ref[n % ntile]) for n, s in enumerate(scores)]
    for n, p in enumerate(weights):
        b, t = divmod(n, ntile)
        o_ref[b * tq:(b + 1) * tq, t * LANES:(t + 1) * LANES] = _weighted_values(
            p, vvs[b][t // 2], tail, lo_half, tq)


def _swa_sample(attn, cache_k, cache_v, l, bias, sink_rows, *, batch, tq, nb):
    m = attn.shape[0]
    qw = N_HEADS_A * HEAD_DIM_A
    kvw = N_KV_A * HEAD_DIM_A
    past = cache_k.shape[2]
    return pl.pallas_call(
        functools.partial(_swa_sample_body, tq=tq, nb=nb),
        grid=(batch // nb,),
        in_specs=[pl.BlockSpec((nb * tq, qw), lambda b: (b, 0)),
                  pl.BlockSpec((nb * tq, 2 * kvw), lambda b: (b, qw // (2 * kvw))),
                  pl.BlockSpec((None, nb, past, kvw), lambda b: (l, b, 0, 0)),
                  pl.BlockSpec((None, nb, past, kvw), lambda b: (l, b, 0, 0)),
                  _resident(bias.shape), _resident(sink_rows.shape)],
        out_specs=[pl.BlockSpec((nb * tq, qw), lambda b: (b, 0)),
                   pl.BlockSpec((nb, tq, kvw), lambda b: (b, 0, 0)),
                   pl.BlockSpec((nb, tq, kvw), lambda b: (b, 0, 0))],
        out_shape=[jax.ShapeDtypeStruct((m, qw), BF16),
                   jax.ShapeDtypeStruct((batch, tq, kvw), F32),
                   jax.ShapeDtypeStruct((batch, tq, kvw), F32)],
        compiler_params=_cparams("parallel"),
        name="swa_sample",
    )(attn, attn, cache_k, cache_v, bias, sink_rows)


def _log_sigmoid(z):
    return -(jnp.maximum(-z, 0.0) + jnp.log(1.0 + jnp.exp(-jnp.abs(z))))


def _gla_body(*refs, tq, cs, has_init):
    if has_init:
        (qk_ref, v_ref, a1_ref, wa2_ref, ba_ref, s0_ref, o_ref, sout_ref,
         st_s, cum_s, qin_s, kout_s, qs_s, ks_s, dec_s, att_s, oi_s, upd_s) = refs
    else:
        (qk_ref, v_ref, a1_ref, wa2_ref, ba_ref, o_ref, sout_ref,
         st_s, cum_s, qin_s, kout_s, qs_s, ks_s, dec_s, att_s, oi_s, upd_s) = refs
    i = pl.program_id(1)
    kw = N_HEADS_B * DK_B
    nchunk = tq // cs
    nsb = cs // GLA_BLOCK
    scale = DK_B ** -0.5
    chunk_rows = [slice(c * cs, (c + 1) * cs) for c in range(nchunk)]
    ksls = [slice(h * DK_B, (h + 1) * DK_B) for h in range(N_HEADS_B)]
    vsls = [slice(h * DV_B, (h + 1) * DV_B) for h in range(N_HEADS_B)]

    @pl.when(i == 0)
    def _():
        for h in range(N_HEADS_B):
            if has_init:
                st_s[h] = s0_ref[0, h].T
            else:
                st_s[h] = jnp.zeros((DV_B, DK_B), F32)

    z = jnp.dot(a1_ref[...], wa2_ref[...], preferred_element_type=F32) + ba_ref[...]
    la = _log_sigmoid(z) * (1.0 / GATE_TAU)
    row = lax.broadcasted_iota(jnp.int32, (cs, cs), 0)
    col = lax.broadcasted_iota(jnp.int32, (cs, cs), 1)
    causal = col <= row
    tri = jnp.where(causal, 1.0, 0.0).astype(BF16)
    la_hi = la.astype(BF16)
    la_lo = (la - la_hi.astype(F32)).astype(BF16)

    def stage1(c):
        rows = chunk_rows[c]
        cum_s[rows, :] = (jnp.dot(tri, la_hi[rows], preferred_element_type=F32)
                          + jnp.dot(tri, la_lo[rows], preferred_element_type=F32))

    def stage2(c):
        rows = chunk_rows[c]
        cum = cum_s[rows, :]
        q = qk_ref[rows, 0:kw].astype(F32)
        k = qk_ref[rows, kw:2 * kw].astype(F32)
        last = cum[cs - 1:cs, :]
        qin_s[rows, :] = (q * (jnp.exp(cum) * scale)).astype(BF16)
        kout_s[rows, :] = (k * jnp.exp(last - cum)).astype(BF16)
        dec_s[c] = jnp.exp(last)
        for sb in range(nsb):
            lo, hi = sb * GLA_BLOCK, (sb + 1) * GLA_BLOCK
            ref_row = cum[lo - 1:lo, :] if sb > 0 else jnp.zeros_like(last)
            qs_s[c * cs + lo:c * cs + hi, :] = (q[lo:hi] * (jnp.exp(cum[lo:hi] - ref_row) * scale)).astype(BF16)
            ks_s[sb, c * cs:c * cs + hi, :] = (k[0:hi] * jnp.exp(ref_row - cum[0:hi])).astype(BF16)
            if hi < cs:
                ks_s[sb, c * cs + hi:(c + 1) * cs, :] = jnp.zeros((cs - hi, kw), BF16)

    def stage3a(c):
        rows = chunk_rows[c]
        for h in range(N_HEADS_B):
            att = jnp.concatenate(
                [lax.dot_general(qs_s[c * cs + sb * GLA_BLOCK:c * cs + (sb + 1) * GLA_BLOCK, ksls[h]],
                                 ks_s[sb, rows, ksls[h]], NT, preferred_element_type=F32) for sb in range(nsb)],
                axis=0)
            att_s[c, h] = jnp.where(causal, att, 0.0).astype(BF16)

    def stage3b(c):
        rows = chunk_rows[c]
        for h in range(N_HEADS_B):
            oi_s[rows, vsls[h]] = jnp.dot(att_s[c, h], v_ref[rows, vsls[h]], preferred_element_type=F32)
            upd_s[c, h] = lax.dot_general(v_ref[rows, vsls[h]], kout_s[rows, ksls[h]], TN,
                                          preferred_element_type=F32)

    def stage4(c):
        rows = chunk_rows[c]
        dec = dec_s[c]
        for h in range(N_HEADS_B):
            st = st_s[h]
            o = oi_s[rows, vsls[h]] + lax.dot_general(qin_s[rows, ksls[h]], st.astype(BF16), NT,
                                                      preferred_element_type=F32)
            o_ref[rows, vsls[h]] = o.astype(BF16)
            st_s[h] = st * dec[:, ksls[h]] + upd_s[c, h]

    stages = (stage1, stage2, stage3a, stage3b, stage4)
    for n in range(nchunk + len(stages) - 1):
        for depth, stage in enumerate(stages):
            if 0 <= n - depth < nchunk:
                stage(n - depth)

    @pl.when(i == pl.num_programs(1) - 1)
    def _():
        for h in range(N_HEADS_B):
            sout_ref[0, h] = st_s[h].T


def _gla(gla_in, wa2p, ba, s0, l, *, batch, seq, tq, cs):
    m = gla_in.shape[0]
    nt = seq // tq
    kw = N_HEADS_B * DK_B
    vw = N_HEADS_B * DV_B
    row = lambda b, i: b * nt + i
    has_init = s0 is not None
    in_specs = [pl.BlockSpec((tq, 2 * kw), lambda b, i: (row(b, i), 0)),
                pl.BlockSpec((tq, vw), lambda b, i: (row(b, i), 2 * kw // vw)),
                pl.BlockSpec((tq, A1_PAD), lambda b, i: (row(b, i), (2 * kw + vw) // A1_PAD)),
                _layer(wa2p, l), _resident(ba.shape)]
    args = [gla_in, gla_in, gla_in, wa2p, ba]
    state_spec = pl.BlockSpec((1, N_HEADS_B, DK_B, DV_B), lambda b, i: (b, 0, 0, 0))
    if has_init:
        in_specs.append(pl.BlockSpec((None, 1, N_HEADS_B, DK_B, DV_B), lambda b, i: (l, b, 0, 0, 0)))
        args.append(s0)
    nchunk = tq // cs
    return pl.pallas_call(
        functools.partial(_gla_body, tq=tq, cs=cs, has_init=has_init),
        grid=(batch, nt),
        in_specs=in_specs,
        out_specs=[pl.BlockSpec((tq, vw), lambda b, i: (row(b, i), 0)), state_spec],
        out_shape=[jax.ShapeDtypeStruct((m, vw), BF16),
                   jax.ShapeDtypeStruct((batch, N_HEADS_B, DK_B, DV_B), F32)],
        scratch_shapes=[pltpu.VMEM((N_HEADS_B, DV_B, DK_B), F32),
                        pltpu.VMEM((tq, kw), F32),
                        pltpu.VMEM((tq, kw), BF16),
                        pltpu.VMEM((tq, kw), BF16),
                        pltpu.VMEM((tq, kw), BF16),
                        pltpu.VMEM((cs // GLA_BLOCK, tq, kw), BF16),
                        pltpu.VMEM((nchunk, 1, kw), F32),
                        pltpu.VMEM((nchunk, N_HEADS_B, cs, cs), BF16),
                        pltpu.VMEM((tq, vw), F32),
                        pltpu.VMEM((nchunk, N_HEADS_B, DV_B, DK_B), F32)],
        compiler_params=_cparams("parallel", "arbitrary"),
        name="gla_init" if has_init else "gla",
    )(*args)


def _merge_ffn_body(x_ref, oa_ref, ob_ref, rb_ref, ga_ref, gb_ref, gob_ref, wo_ref, gf_ref,
                    wg_ref, wu_ref, wd_ref, y_ref, *, ff_chunks, row_parts):
    tm = x_ref.shape[0]
    parts = [slice(k * tm // row_parts, (k + 1) * tm // row_parts) for k in range(row_parts)]
    gob = gob_ref[...]

    def mix2(rows):
        ob = ob_ref[rows, :].astype(F32)
        obn = jnp.concatenate(
            [_rms_rows(ob[:, h * DV_B:(h + 1) * DV_B], gob) for h in range(N_HEADS_B)], axis=1)
        rbh = rb_ref[rows, :].astype(F32)
        obn = obn * (rbh * (jnp.tanh(rbh) + 1.0))
        return ((jnp.tanh(ga_ref[rows, :].astype(F32)) + 1.0) * oa_ref[rows, :].astype(F32)
                + (jnp.tanh(gb_ref[rows, :].astype(F32)) + 1.0) * obn).astype(BF16)

    mixes = [mix2(rows) for rows in parts]
    acc = [x_ref[rows, :] + jnp.dot(mx, wo_ref[...], preferred_element_type=F32)
           for rows, mx in zip(parts, mixes)]
    hs = [_rms_rows(x1, gf_ref[...]).astype(BF16) for x1 in acc]
    units = [(c0, c1, k) for c0, c1 in ff_chunks for k in range(row_parts)]
    act = {}
    for n in range(len(units) + DOWN_LAG):
        if n < len(units):
            c0, c1, k = units[n]
            gh = jnp.dot(hs[k], wg_ref[:, c0:c1], preferred_element_type=F32)
            u = jnp.dot(hs[k], wu_ref[:, c0:c1], preferred_element_type=F32)
            act[n] = ((gh * (jnp.tanh(gh) + 1.0)) * u).astype(BF16)
        if n >= DOWN_LAG:
            c0, c1, k = units[n - DOWN_LAG]
            acc[k] = acc[k] + jnp.dot(act.pop(n - DOWN_LAG), wd_ref[c0:c1, :], preferred_element_type=F32)
    for rows, y in zip(parts, acc):
        y_ref[rows, :] = y


def _ff_chunks(f, pref):
    mxu = 2 * LANES
    step = pref if f % mxu == 0 else f
    edges = list(range(0, f, step)) + [f]
    return tuple(zip(edges[:-1], edges[1:]))


def _merge_ffn(x, oa, ob, gates, gob, wo, gf, wg, wu, wd, l, *, tm):
    m, d = x.shape
    tile = lambda c: pl.BlockSpec((tm, d), lambda i: (i, c))
    return pl.pallas_call(
        functools.partial(_merge_ffn_body, ff_chunks=_ff_chunks(wg.shape[2], 1024),
                          row_parts=2 if tm % 32 == 0 else 1),
        grid=(m // tm,),
        in_specs=[tile(0), tile(0), tile(0), tile(0), tile(1), tile(2),
                  _resident(gob.shape), _layer(wo, l), _resident(gf.shape),
                  _layer(wg, l), _layer(wu, l), _layer(wd, l)],
        out_specs=tile(0),
        out_shape=jax.ShapeDtypeStruct((m, d), F32),
        compiler_params=_cparams("parallel"),
        name="merge_ffn",
    )(x, oa, ob, gates, gates, gates, gob, wo, gf, wg, wu, wd)


def _alibi_bias_t(cq, band, q_off, masked_prefixes):
    slopes = LOG2E * 2.0 ** (-8.0 * np.arange(1, N_HEADS_A + 1, dtype=np.float64) / N_HEADS_A)
    dist = np.abs(np.arange(band)[:, None] - q_off - np.arange(cq)[None, :])
    bias = -slopes.reshape(N_HEADS_A // 2, 1, 2, 1) * dist[None, :, None, :]
    bias = bias.reshape(N_HEADS_A // 2, band, 2 * cq)
    out = np.stack([np.where(np.arange(band)[None, :, None] < p, NEG, bias) for p in masked_prefixes])
    return jnp.asarray(out, F32)


def _sink_rows(sinks, cq):
    depth = sinks.shape[0]
    return jnp.repeat(sinks.reshape(depth, N_HEADS_A // 2, 2) * LOG2E, cq, axis=2)[:, :, None, :]


def _row_tile(m, pref):
    t = min(pref, m)
    while m % t:
        t //= 2
    return t


def kernel(x_prompt, x_sample, cache_swa_k, cache_swa_v, state_gla, w_in, w_a2, b_a, g_q, g_k, sinks,
           g_ob, w_out, g_mix, g_ffn, w_gate, w_up, w_down):
    depth = w_in.shape[0]
    bp, tp, d = x_prompt.shape
    bs, ts, _ = x_sample.shape
    past = cache_swa_k.shape[2]
    qw, kvw = N_HEADS_A * HEAD_DIM_A, N_KV_A * HEAD_DIM_A
    kw, vw = N_HEADS_B * DK_B, N_HEADS_B * DV_B
    assert past == WINDOW and tp % CHUNK == 0 and ts % GLA_BLOCK == 0 and ts <= CHUNK

    rb0 = qw + 2 * kvw + 2 * kw + vw
    a10 = rb0 + vw
    col_scale = np.ones((w_in.shape[2],), np.float32)
    col_scale[rb0:a10] = 0.5
    col_scale[a10 + GATE_RANK:] = 0.5
    w_main = (w_in * col_scale).astype(BF16)
    w_mg = w_main[:, :, a10 + GATE_RANK:]
    w_a2p = jnp.concatenate(
        [w_a2, jnp.zeros((depth, A1_PAD - GATE_RANK, kw), w_a2.dtype)], axis=1).astype(BF16)
    w_out_b, w_gate_b, w_up_b, w_down_b = (w.astype(BF16) for w in (0.5 * w_out, 0.5 * w_gate, w_up, w_down))
    gq2 = jnp.tile(g_q * (HEAD_DIM_A ** -0.5 * LOG2E), (1, LANES // HEAD_DIM_A))[:, None, :]
    gk2 = jnp.tile(g_k, (1, LANES // HEAD_DIM_A))[:, None, :]
    gob4 = g_ob[:, None, :]
    bias_p = _alibi_bias_t(CHUNK, WINDOW + CHUNK, WINDOW, (0, WINDOW, WINDOW - CHUNK))
    bias_s = _alibi_bias_t(ts, past + ts, past, (0,))
    sink_p = _sink_rows(sinks, CHUNK)
    sink_s = _sink_rows(sinks, ts)

    tq_p = _row_tile(tp, 512)
    tm_p = _row_tile(bp * tp, 512)
    tm_s = _row_tile(bs * ts, 512)

    xp = x_prompt.reshape(bp * tp, d)
    xs = x_sample.reshape(bs * ts, d)
    pk, pv, ps, sk, sv, ss = [], [], [], [], [], []
    cache_k = cache_swa_k.reshape(depth, bs, past, kvw)
    cache_v = cache_swa_v.reshape(depth, bs, past, kvw)
    for l in range(depth):
        dense = dict(gob=gob4[l], wo=w_out_b, gf=g_ffn[l][None], wg=w_gate_b, wu=w_up_b, wd=w_down_b, l=l)
        attn, gla_in, gates = _inproj(xp, g_mix[l][None], gq2[l], gk2[l], w_main, w_mg, l, tm=tm_p)
        oa, kl, vl = _swa_prompt(attn, bias_p, sink_p[l], batch=bp, seq=tp, tq=tq_p)
        ob, st = _gla(gla_in, w_a2p, b_a[l][None], None, l, batch=bp, seq=tp, tq=tq_p, cs=CHUNK)
        xp = _merge_ffn(xp, oa, ob, gates, tm=tm_p, **dense)
        pk.append(kl.reshape(bp, WINDOW, N_KV_A, HEAD_DIM_A))
        pv.append(vl.reshape(bp, WINDOW, N_KV_A, HEAD_DIM_A))
        ps.append(st)
        attn, gla_in, gates = _inproj(xs, g_mix[l][None], gq2[l], gk2[l], w_main, w_mg, l, tm=tm_s)
        oa, kn, vn = _swa_sample(attn, cache_k, cache_v, l, bias_s, sink_s[l], batch=bs, tq=ts, nb=_row_tile(bs, 4))
        ob, st = _gla(gla_in, w_a2p, b_a[l][None], state_gla, l, batch=bs, seq=ts, tq=ts, cs=ts)
        xs = _merge_ffn(xs, oa, ob, gates, tm=tm_s, **dense)
        sk.append(kn.reshape(bs, ts, N_KV_A, HEAD_DIM_A))
        sv.append(vn.reshape(bs, ts, N_KV_A, HEAD_DIM_A))
        ss.append(st)

    return (xp.reshape(bp, tp, d), xs.reshape(bs, ts, d), jnp.stack(pk), jnp.stack(pv), jnp.stack(ps),
            jnp.stack(sk), jnp.stack(sv), jnp.stack(ss))
```

```python
import functools

import jax
import jax.numpy as jnp
import numpy as np
from jax import lax
from jax.experimental import pallas as pl
from jax.experimental.pallas import tpu as pltpu

CHUNK = 64
EPS = 1e-6
NEG = -1e30
N_HEADS_A = 16
N_KV_A = 4
GROUP_A = N_HEADS_A // N_KV_A
HEAD_DIM_A = 64
WINDOW = 128
N_HEADS_B = 4
DK_B = 128
DV_B = 256
GATE_RANK = 16
GATE_TAU = 16.0
GLA_BLOCK = 16

LANES = 128
A1_PAD = LANES
VMEM_LIMIT = 56 * 1024 * 1024
SCORE_RING = 8
PROB_RING = 6
PAD_ROWS = 16
DOWN_LAG = 2
LOG2E = 1.4426950408889634

BF16 = jnp.bfloat16
F32 = jnp.float32
NT = (((1,), (1,)), ((), ()))
TN = (((0,), (0,)), ((), ()))


def _cparams(*sem):
    return pltpu.CompilerParams(dimension_semantics=sem, vmem_limit_bytes=VMEM_LIMIT)


def _resident(shape):
    nd = len(shape)
    return pl.BlockSpec(shape, lambda *_: (0,) * nd, pipeline_mode=pl.Buffered(1))


def _layer(stacked, l):
    nd = stacked.ndim - 1
    return pl.BlockSpec((None,) + stacked.shape[1:], lambda *_: (l,) + (0,) * nd, pipeline_mode=pl.Buffered(1))


def _rms_rows(x, g):
    ms = jnp.mean(x * x, axis=-1, keepdims=True)
    return (x * lax.rsqrt(ms + EPS)) * g


def _head_norm_tile(x, g2, lo_half):
    sq = x * x
    ms_lo = jnp.sum(jnp.where(lo_half, sq, 0.0), axis=-1, keepdims=True) * (1.0 / HEAD_DIM_A)
    ms_hi = jnp.sum(jnp.where(lo_half, 0.0, sq), axis=-1, keepdims=True) * (1.0 / HEAD_DIM_A)
    r = jnp.where(lo_half, lax.rsqrt(ms_lo + EPS), lax.rsqrt(ms_hi + EPS))
    return (x * r) * g2


def _inproj_body(x_ref, g_ref, gq_ref, gk_ref, wm_ref, wg_ref, oa_ref, ob_ref, oc_ref, *, plan):
    h = _rms_rows(x_ref[...], g_ref[...]).astype(BF16)
    lo_half = lax.broadcasted_iota(jnp.int32, (1, LANES), 1) < HEAD_DIM_A
    qw, kvw = N_HEADS_A * HEAD_DIM_A, N_KV_A * HEAD_DIM_A
    w_refs, o_refs = (wm_ref, wg_ref), (oa_ref, ob_ref, oc_ref)
    for src, c0, c1, dst, d0 in plan:
        acc = jnp.dot(h, w_refs[src][:, c0:c1], preferred_element_type=F32)
        if dst != 0:
            o_refs[dst][:, d0:d0 + (c1 - c0)] = acc.astype(BF16)
            continue
        for t0 in range(0, c1 - c0, LANES):
            col = d0 + t0
            tile = acc[:, t0:t0 + LANES]
            if col < qw:
                tile = _head_norm_tile(tile, gq_ref[...], lo_half)
            elif col < qw + kvw:
                tile = _head_norm_tile(tile, gk_ref[...], lo_half)
            oa_ref[:, col:col + LANES] = tile.astype(BF16)


def _inproj_plan(d, col_chunk):
    qw, kvw = N_HEADS_A * HEAD_DIM_A, N_KV_A * HEAD_DIM_A
    kw, vw = N_HEADS_B * DK_B, N_HEADS_B * DV_B
    attn_w, gla_w = qw + 2 * kvw, 2 * kw + vw
    runs = [(0, 0, attn_w, 0, 0),
            (0, attn_w, attn_w + gla_w, 1, 0),
            (0, attn_w + gla_w + vw, attn_w + gla_w + vw + A1_PAD, 1, gla_w),
            (0, attn_w + gla_w, attn_w + gla_w + vw, 2, 0),
            (1, 0, 2 * d, 2, vw)]
    plan = []
    for src, c0, c1, dst, d0 in runs:
        for s0 in range(c0, c1, col_chunk):
            plan.append((src, s0, min(s0 + col_chunk, c1), dst, d0 + s0 - c0))
    return tuple(plan), (attn_w, gla_w + A1_PAD, vw + 2 * d)


def _inproj(x, g, gq2, gk2, w_main, w_mg, l, *, tm):
    m, d = x.shape
    plan, widths = _inproj_plan(d, 512)
    return pl.pallas_call(
        functools.partial(_inproj_body, plan=plan),
        grid=(m // tm,),
        in_specs=[pl.BlockSpec((tm, d), lambda i: (i, 0)), _resident(g.shape),
                  _resident(gq2.shape), _resident(gk2.shape), _layer(w_main, l), _layer(w_mg, l)],
        out_specs=[pl.BlockSpec((tm, w), lambda i: (i, 0)) for w in widths],
        out_shape=[jax.ShapeDtypeStruct((m, w), BF16) for w in widths],
        compiler_params=_cparams("parallel"),
        name="inproj",
    )(x, g, gq2, gk2, w_main, w_mg)


def _dup_halves(x, lo_half):
    r = pltpu.roll(x, HEAD_DIM_A, axis=1)
    return jnp.where(lo_half, x, r), jnp.where(lo_half, r, x)


def _scores(lhs, kk, bias_t):
    return lax.dot_general(kk, lhs, NT, preferred_element_type=F32) + bias_t


def _sink_weights(s, sink_row):
    m = jnp.maximum(jnp.max(s, axis=0, keepdims=True), sink_row)
    p = jnp.exp2(s - m).astype(BF16)
    first = lax.broadcasted_iota(jnp.int32, (PAD_ROWS, s.shape[1]), 0) == 0
    tail = jnp.where(first, jnp.exp2(sink_row - m), 0.0).astype(BF16)
    return jnp.concatenate([p, tail], axis=0)


def _value_operand_tail(band):
    rows = band + PAD_ROWS
    counted = lax.broadcasted_iota(jnp.int32, (rows, LANES), 0) <= band
    return jnp.zeros((PAD_ROWS, LANES), BF16), jnp.where(counted, 1.0, 0.0).astype(BF16)


def _weighted_values(p, vv, tail, lo_half, cq):
    zero_rows, ones_blk = tail
    rhs = jnp.concatenate([jnp.concatenate([vv, zero_rows], axis=0), ones_blk], axis=1)
    o = lax.dot_general(p, rhs, TN, preferred_element_type=F32)
    num = jnp.where(lo_half, o[0:cq, 0:LANES], o[cq:2 * cq, 0:LANES])
    den = jnp.where(lo_half, o[0:cq, LANES:], o[cq:2 * cq, LANES:])
    return (num / den).astype(BF16)


def _half_masks():
    lane = lax.broadcasted_iota(jnp.int32, (1, LANES), 1)
    lo_half = lane < HEAD_DIM_A
    return lo_half, jnp.where(lo_half, 1.0, 0.0).astype(BF16), jnp.where(lo_half, 0.0, 1.0).astype(BF16)


def _swa_prompt_body(q_ref, kv_ref, bias_ref, sink_ref,
                     o_ref, klast_ref, vlast_ref, kk_s, vv_s, qm_s, s_buf, p_buf, *, tq):
    i = pl.program_id(1)
    kvw = N_KV_A * HEAD_DIM_A
    nchunk = tq // CHUNK
    lo_half, mask_lo, mask_hi = _half_masks()

    @pl.when(i == 0)
    def _():
        kk_s[:, 0:WINDOW, :] = jnp.zeros((N_KV_A, WINDOW, LANES), BF16)
        vv_s[:, 0:WINDOW, :] = jnp.zeros((N_KV_A, WINDOW, LANES), BF16)

    @pl.when(i > 0)
    def _():
        kk_s[:, 0:WINDOW, :] = kk_s[:, tq:tq + WINDOW, :]
        vv_s[:, 0:WINDOW, :] = vv_s[:, tq:tq + WINDOW, :]

    for t in range(kvw // LANES):
        k_t = kv_ref[:, t * LANES:(t + 1) * LANES].astype(F32)
        v_t = kv_ref[:, kvw + t * LANES:kvw + (t + 1) * LANES].astype(F32)
        klast_ref[0, :, t * LANES:(t + 1) * LANES] = k_t[tq - WINDOW:, :]
        vlast_ref[0, :, t * LANES:(t + 1) * LANES] = v_t[tq - WINDOW:, :]
        k0, k1 = _dup_halves(k_t, lo_half)
        v0, v1 = _dup_halves(v_t, lo_half)
        kk_s[2 * t, WINDOW:, :] = k0.astype(BF16)
        kk_s[2 * t + 1, WINDOW:, :] = k1.astype(BF16)
        vv_s[2 * t, WINDOW:, :] = v0.astype(BF16)
        vv_s[2 * t + 1, WINDOW:, :] = v1.astype(BF16)

    for t in range(N_HEADS_A * HEAD_DIM_A // LANES):
        q_t = q_ref[:, t * LANES:(t + 1) * LANES]
        qm_s[t, :, 0:CHUNK, :] = (q_t * mask_lo).reshape(nchunk, CHUNK, LANES)
        qm_s[t, :, CHUNK:, :] = (q_t * mask_hi).reshape(nchunk, CHUNK, LANES)

    band = WINDOW + CHUNK
    tail = _value_operand_tail(band)
    ntile = N_HEADS_A * HEAD_DIM_A // LANES
    nitem = nchunk * ntile
    n_sring, n_pring = s_buf.shape[0], p_buf.shape[0]
    skew_sm, skew_pv = n_sring - 2, n_pring - 1

    def keys(n):
        c, t = divmod(n, ntile)
        return t // 2, slice(c * CHUNK, c * CHUNK + band)

    def stage_scores(n):
        c, t = divmod(n, ntile)
        j, rows = keys(n)
        variant = jnp.where(i == 0, c + 1, 0) if c < WINDOW // CHUNK else 0
        s_buf[n % n_sring] = _scores(qm_s[t, c], kk_s[j, rows, :], bias_ref[variant, t])

    def stage_softmax(n):
        p_buf[n % n_pring] = _sink_weights(s_buf[n % n_sring], sink_ref[n % ntile])

    def stage_values(n):
        c, t = divmod(n, ntile)
        j, rows = keys(n)
        o_ref[c * CHUNK:(c + 1) * CHUNK, t * LANES:(t + 1) * LANES] = _weighted_values(
            p_buf[n % n_pring], vv_s[j, rows, :], tail, lo_half, CHUNK)

    for n in range(nitem + skew_sm + skew_pv):
        if n < nitem:
            stage_scores(n)
        if 0 <= n - skew_sm < nitem:
            stage_softmax(n - skew_sm)
        if 0 <= n - skew_sm - skew_pv < nitem:
            stage_values(n - skew_sm - skew_pv)


def _swa_prompt(attn, bias, sink_rows, *, batch, seq, tq):
    m = attn.shape[0]
    nt = seq // tq
    qw = N_HEADS_A * HEAD_DIM_A
    kvw = N_KV_A * HEAD_DIM_A
    row = lambda b, i: b * nt + i
    return pl.pallas_call(
        functools.partial(_swa_prompt_body, tq=tq),
        grid=(batch, nt),
        in_specs=[pl.BlockSpec((tq, qw), lambda b, i: (row(b, i), 0)),
                  pl.BlockSpec((tq, 2 * kvw), lambda b, i: (row(b, i), qw // (2 * kvw))),
                  _resident(bias.shape), _resident(sink_rows.shape)],
        out_specs=[pl.BlockSpec((tq, qw), lambda b, i: (row(b, i), 0)),
                   pl.BlockSpec((1, WINDOW, kvw), lambda b, i: (b, 0, 0)),
                   pl.BlockSpec((1, WINDOW, kvw), lambda b, i: (b, 0, 0))],
        out_shape=[jax.ShapeDtypeStruct((m, qw), BF16),
                   jax.ShapeDtypeStruct((batch, WINDOW, kvw), F32),
                   jax.ShapeDtypeStruct((batch, WINDOW, kvw), F32)],
        scratch_shapes=[pltpu.VMEM((N_KV_A, WINDOW + tq, LANES), BF16),
                        pltpu.VMEM((N_KV_A, WINDOW + tq, LANES), BF16),
                        pltpu.VMEM((qw // LANES, tq // CHUNK, 2 * CHUNK, LANES), BF16),
                        pltpu.VMEM((SCORE_RING, WINDOW + CHUNK, LANES), F32),
                        pltpu.VMEM((PROB_RING, WINDOW + CHUNK + PAD_ROWS, LANES), BF16)],
        compiler_params=_cparams("parallel", "arbitrary"),
        name="swa_prompt",
    )(attn, attn, bias, sink_rows)


def _swa_sample_body(q_ref, kv_ref, ck_ref, cv_ref, bias_ref, sink_ref,
                     o_ref, knew_ref, vnew_ref, *, tq, nb):
    kvw = N_KV_A * HEAD_DIM_A
    lo_half, mask_lo, mask_hi = _half_masks()
    tail = _value_operand_tail(ck_ref.shape[1] + tq)
    ntile = N_HEADS_A * HEAD_DIM_A // LANES
    vvs, scores = [], []
    for b in range(nb):
        rows = slice(b * tq, (b + 1) * tq)
        kk, vv = [None] * N_KV_A, [None] * N_KV_A
        for t in range(kvw // LANES):
            sl = slice(t * LANES, (t + 1) * LANES)
            k_t = kv_ref[rows, sl].astype(F32)
            v_t = kv_ref[rows, kvw + t * LANES:kvw + (t + 1) * LANES].astype(F32)
            knew_ref[b, :, sl] = k_t
            vnew_ref[b, :, sl] = v_t
            k0, k1 = _dup_halves(jnp.concatenate([ck_ref[b, :, sl], k_t], axis=0), lo_half)
            v0, v1 = _dup_halves(jnp.concatenate([cv_ref[b, :, sl], v_t], axis=0), lo_half)
            kk[2 * t], kk[2 * t + 1] = k0.astype(BF16), k1.astype(BF16)
            vv[2 * t], vv[2 * t + 1] = v0.astype(BF16), v1.astype(BF16)
        vvs.append(vv)
        for t in range(ntile):
            q_t = q_ref[rows, t * LANES:(t + 1) * LANES]
            lhs = jnp.concatenate([q_t * mask_lo, q_t * mask_hi], axis=0)
            scores.append(_scores(lhs, kk[t // 2], bias_ref[0, t]))
    weights = [_sink_weights(s, sink_ref[n % ntile]) for n, s in enumerate(scores)]
    for n, p in enumerate(weights):
        b, t = divmod(n, ntile)
        o_ref[b * tq:(b + 1) * tq, t * LANES:(t + 1) * LANES] = _weighted_values(
            p, vvs[b][t // 2], tail, lo_half, tq)


def _swa_sample(attn, cache_k, cache_v, l, bias, sink_rows, *, batch, tq, nb):
    m = attn.shape[0]
    qw = N_HEADS_A * HEAD_DIM_A
    kvw = N_KV_A * HEAD_DIM_A
    past = cache_k.shape[2]
    return pl.pallas_call(
        functools.partial(_swa_sample_body, tq=tq, nb=nb),
        grid=(batch // nb,),
        in_specs=[pl.BlockSpec((nb * tq, qw), lambda b: (b, 0)),
                  pl.BlockSpec((nb * tq, 2 * kvw), lambda b: (b, qw // (2 * kvw))),
                  pl.BlockSpec((None, nb, past, kvw), lambda b: (l, b, 0, 0)),
                  pl.BlockSpec((None, nb, past, kvw), lambda b: (l, b, 0, 0)),
                  _resident(bias.shape), _resident(sink_rows.shape)],
        out_specs=[pl.BlockSpec((nb * tq, qw), lambda b: (b, 0)),
                   pl.BlockSpec((nb, tq, kvw), lambda b: (b, 0, 0)),
                   pl.BlockSpec((nb, tq, kvw), lambda b: (b, 0, 0))],
        out_shape=[jax.ShapeDtypeStruct((m, qw), BF16),
                   jax.ShapeDtypeStruct((batch, tq, kvw), F32),
                   jax.ShapeDtypeStruct((batch, tq, kvw), F32)],
        compiler_params=_cparams("parallel"),
        name="swa_sample",
    )(attn, attn, cache_k, cache_v, bias, sink_rows)


def _log_sigmoid(z):
    return -(jnp.maximum(-z, 0.0) + jnp.log(1.0 + jnp.exp(-jnp.abs(z))))


def _gla_body(*refs, tq, cs, has_init, has_alias):
    refs = list(refs)
    qk_ref, v_ref, a1_ref, wa2_ref, ba_ref = refs[:5]
    s0_ref = refs[5] if has_init else None
    (o_ref, sout_ref, st_s, cum_s, qin_s, kout_s, qs_s, ks_s, dec_s, att_s, oi_s,
     upd_s) = refs[5 + has_init + has_alias:]
    i = pl.program_id(1)
    kw = N_HEADS_B * DK_B
    nchunk = tq // cs
    nsb = cs // GLA_BLOCK
    scale = DK_B ** -0.5
    chunk_rows = [slice(c * cs, (c + 1) * cs) for c in range(nchunk)]
    ksls = [slice(h * DK_B, (h + 1) * DK_B) for h in range(N_HEADS_B)]
    vsls = [slice(h * DV_B, (h + 1) * DV_B) for h in range(N_HEADS_B)]

    @pl.when(i == 0)
    def _():
        for h in range(N_HEADS_B):
            if has_init:
                st_s[h] = s0_ref[0, h].T
            else:
                st_s[h] = jnp.zeros((DV_B, DK_B), F32)

    z = jnp.dot(a1_ref[...], wa2_ref[...], preferred_element_type=F32) + ba_ref[...]
    la = _log_sigmoid(z) * (1.0 / GATE_TAU)
    row = lax.broadcasted_iota(jnp.int32, (cs, cs), 0)
    col = lax.broadcasted_iota(jnp.int32, (cs, cs), 1)
    causal = col <= row
    tri = jnp.where(causal, 1.0, 0.0).astype(BF16)
    la_hi = la.astype(BF16)
    la_lo = (la - la_hi.astype(F32)).astype(BF16)

    def stage1(c):
        rows = chunk_rows[c]
        cum_s[rows, :] = (jnp.dot(tri, la_hi[rows], preferred_element_type=F32)
                          + jnp.dot(tri, la_lo[rows], preferred_element_type=F32))

    def stage2(c):
        rows = chunk_rows[c]
        cum = cum_s[rows, :]
        q = qk_ref[rows, 0:kw].astype(F32)
        k = qk_ref[rows, kw:2 * kw].astype(F32)
        last = cum[cs - 1:cs, :]
        qin_s[rows, :] = (q * (jnp.exp(cum) * scale)).astype(BF16)
        kout_s[rows, :] = (k * jnp.exp(last - cum)).astype(BF16)
        dec_s[c] = jnp.exp(last)
        for sb in range(nsb):
            lo, hi = sb * GLA_BLOCK, (sb + 1) * GLA_BLOCK
            ref_row = cum[lo - 1:lo, :] if sb > 0 else jnp.zeros_like(last)
            qs_s[c * cs + lo:c * cs + hi, :] = (q[lo:hi] * (jnp.exp(cum[lo:hi] - ref_row) * scale)).astype(BF16)
            ks_s[sb, c * cs:c * cs + hi, :] = (k[0:hi] * jnp.exp(ref_row - cum[0:hi])).astype(BF16)
            if hi < cs:
                ks_s[sb, c * cs + hi:(c + 1) * cs, :] = jnp.zeros((cs - hi, kw), BF16)

    def stage3a(c):
        rows = chunk_rows[c]
        for h in range(N_HEADS_B):
            att = jnp.concatenate(
                [lax.dot_general(qs_s[c * cs + sb * GLA_BLOCK:c * cs + (sb + 1) * GLA_BLOCK, ksls[h]],
                                 ks_s[sb, rows, ksls[h]], NT, preferred_element_type=F32) for sb in range(nsb)],
                axis=0)
            att_s[c, h] = jnp.where(causal, att, 0.0).astype(BF16)

    def stage3b(c):
        rows = chunk_rows[c]
        for h in range(N_HEADS_B):
            oi_s[rows, vsls[h]] = jnp.dot(att_s[c, h], v_ref[rows, vsls[h]], preferred_element_type=F32)
            upd_s[c, h] = lax.dot_general(v_ref[rows, vsls[h]], kout_s[rows, ksls[h]], TN,
                                          preferred_element_type=F32)

    def stage4(c):
        rows = chunk_rows[c]
        dec = dec_s[c]
        for h in range(N_HEADS_B):
            st = st_s[h]
            o = oi_s[rows, vsls[h]] + lax.dot_general(qin_s[rows, ksls[h]], st.astype(BF16), NT,
                                                      preferred_element_type=F32)
            o_ref[rows, vsls[h]] = o.astype(BF16)
            st_s[h] = st * dec[:, ksls[h]] + upd_s[c, h]

    stages = (stage1, stage2, stage3a, stage3b, stage4)
    for n in range(nchunk + len(stages) - 1):
        for depth, stage in enumerate(stages):
            if 0 <= n - depth < nchunk:
                stage(n - depth)

    @pl.when(i == pl.num_programs(1) - 1)
    def _():
        for h in range(N_HEADS_B):
            sout_ref[0, h] = st_s[h].T


def _gla(gla_in, wa2p, ba, s0, states, l, *, batch, seq, tq, cs):
    depth = wa2p.shape[0]
    m = gla_in.shape[0]
    nt = seq // tq
    kw = N_HEADS_B * DK_B
    vw = N_HEADS_B * DV_B
    row = lambda b, i: b * nt + i
    has_init = s0 is not None
    in_specs = [pl.BlockSpec((tq, 2 * kw), lambda b, i: (row(b, i), 0)),
                pl.BlockSpec((tq, vw), lambda b, i: (row(b, i), 2 * kw // vw)),
                pl.BlockSpec((tq, A1_PAD), lambda b, i: (row(b, i), (2 * kw + vw) // A1_PAD)),
                _layer(wa2p, l), _resident(ba.shape)]
    args = [gla_in, gla_in, gla_in, wa2p, ba]
    state_spec = pl.BlockSpec((None, 1, N_HEADS_B, DK_B, DV_B), lambda b, i: (l, b, 0, 0, 0))
    if has_init:
        in_specs.append(state_spec)
        args.append(s0)
    aliases = {}
    if states is not None:
        aliases[len(args)] = 1
        in_specs.append(pl.BlockSpec(memory_space=pl.ANY))
        args.append(states)
    nchunk = tq // cs
    return pl.pallas_call(
        functools.partial(_gla_body, tq=tq, cs=cs, has_init=has_init, has_alias=states is not None),
        grid=(batch, nt),
        in_specs=in_specs,
        out_specs=[pl.BlockSpec((tq, vw), lambda b, i: (row(b, i), 0)), state_spec],
        out_shape=[jax.ShapeDtypeStruct((m, vw), BF16),
                   jax.ShapeDtypeStruct((depth, batch, N_HEADS_B, DK_B, DV_B), F32)],
        input_output_aliases=aliases,
        scratch_shapes=[pltpu.VMEM((N_HEADS_B, DV_B, DK_B), F32),
                        pltpu.VMEM((tq, kw), F32),
                        pltpu.VMEM((tq, kw), BF16),
                        pltpu.VMEM((tq, kw), BF16),
                        pltpu.VMEM((tq, kw), BF16),
                        pltpu.VMEM((cs // GLA_BLOCK, tq, kw), BF16),
                        pltpu.VMEM((nchunk, 1, kw), F32),
                        pltpu.VMEM((nchunk, N_HEADS_B, cs, cs), BF16),
                        pltpu.VMEM((tq, vw), F32),
                        pltpu.VMEM((nchunk, N_HEADS_B, DV_B, DK_B), F32)],
        compiler_params=_cparams("parallel", "arbitrary"),
        name="gla_init" if has_init else "gla",
    )(*args)


def _merge_ffn_body(x_ref, oa_ref, ob_ref, rb_ref, ga_ref, gb_ref, gob_ref, wo_ref, gf_ref,
                    wg_ref, wu_ref, wd_ref, y_ref, *, ff_chunks, row_parts):
    tm = x_ref.shape[0]
    parts = [slice(k * tm // row_parts, (k + 1) * tm // row_parts) for k in range(row_parts)]
    gob = gob_ref[...]

    def mix2(rows):
        ob = ob_ref[rows, :].astype(F32)
        obn = jnp.concatenate(
            [_rms_rows(ob[:, h * DV_B:(h + 1) * DV_B], gob) for h in range(N_HEADS_B)], axis=1)
        rbh = rb_ref[rows, :].astype(F32)
        obg = (obn * (rbh * (jnp.tanh(rbh) + 1.0))).astype(BF16)
        one = jnp.ones((), BF16)
        return ((jnp.tanh(ga_ref[rows, :]) + one) * oa_ref[rows, :]
                + (jnp.tanh(gb_ref[rows, :]) + one) * obg)

    mixes = [mix2(rows) for rows in parts]
    acc = [x_ref[rows, :] + jnp.dot(mx, wo_ref[...], preferred_element_type=F32)
           for rows, mx in zip(parts, mixes)]
    hs = [_rms_rows(x1, gf_ref[...]).astype(BF16) for x1 in acc]
    units = [(c0, c1, k) for c0, c1 in ff_chunks for k in range(row_parts)]
    act = {}
    for n in range(len(units) + DOWN_LAG):
        if n < len(units):
            c0, c1, k = units[n]
            gh = jnp.dot(hs[k], wg_ref[:, c0:c1], preferred_element_type=F32)
            u = jnp.dot(hs[k], wu_ref[:, c0:c1], preferred_element_type=F32)
            act[n] = ((gh * (jnp.tanh(gh) + 1.0)) * u).astype(BF16)
        if n >= DOWN_LAG:
            c0, c1, k = units[n - DOWN_LAG]
            acc[k] = acc[k] + jnp.dot(act.pop(n - DOWN_LAG), wd_ref[c0:c1, :], preferred_element_type=F32)
    for rows, y in zip(parts, acc):
        y_ref[rows, :] = y


def _ff_chunks(f, pref):
    mxu = 2 * LANES
    step = pref if f % mxu == 0 else f
    edges = list(range(0, f, step)) + [f]
    return tuple(zip(edges[:-1], edges[1:]))


def _merge_ffn(x, oa, ob, gates, gob, wo, gf, wg, wu, wd, l, *, tm):
    m, d = x.shape
    tile = lambda c: pl.BlockSpec((tm, d), lambda i: (i, c))
    return pl.pallas_call(
        functools.partial(_merge_ffn_body, ff_chunks=_ff_chunks(wg.shape[2], 1024),
                          row_parts=2 if tm % 32 == 0 else 1),
        grid=(m // tm,),
        in_specs=[tile(0), tile(0), tile(0), tile(0), tile(1), tile(2),
                  _resident(gob.shape), _layer(wo, l), _resident(gf.shape),
                  _layer(wg, l), _layer(wu, l), _layer(wd, l)],
        out_specs=tile(0),
        out_shape=jax.ShapeDtypeStruct((m, d), F32),
        compiler_params=_cparams("parallel"),
        name="merge_ffn",
    )(x, oa, ob, gates, gates, gates, gob, wo, gf, wg, wu, wd)


def _alibi_bias_t(cq, band, q_off, masked_prefixes):
    slopes = LOG2E * 2.0 ** (-8.0 * np.arange(1, N_HEADS_A + 1, dtype=np.float64) / N_HEADS_A)
    dist = np.abs(np.arange(band)[:, None] - q_off - np.arange(cq)[None, :])
    bias = -slopes.reshape(N_HEADS_A // 2, 1, 2, 1) * dist[None, :, None, :]
    bias = bias.reshape(N_HEADS_A // 2, band, 2 * cq)
    out = np.stack([np.where(np.arange(band)[None, :, None] < p, NEG, bias) for p in masked_prefixes])
    return jnp.asarray(out, F32)


def _sink_rows(sinks, cq):
    depth = sinks.shape[0]
    return jnp.repeat(sinks.reshape(depth, N_HEADS_A // 2, 2) * LOG2E, cq, axis=2)[:, :, None, :]


def _row_tile(m, pref):
    t = min(pref, m)
    while m % t:
        t //= 2
    return t


def kernel(x_prompt, x_sample, cache_swa_k, cache_swa_v, state_gla, w_in, w_a2, b_a, g_q, g_k, sinks,
           g_ob, w_out, g_mix, g_ffn, w_gate, w_up, w_down):
    depth = w_in.shape[0]
    bp, tp, d = x_prompt.shape
    bs, ts, _ = x_sample.shape
    past = cache_swa_k.shape[2]
    qw, kvw = N_HEADS_A * HEAD_DIM_A, N_KV_A * HEAD_DIM_A
    kw, vw = N_HEADS_B * DK_B, N_HEADS_B * DV_B
    assert past == WINDOW and tp % CHUNK == 0 and ts % GLA_BLOCK == 0 and ts <= CHUNK

    rb0 = qw + 2 * kvw + 2 * kw + vw
    a10 = rb0 + vw
    col_scale = np.ones((w_in.shape[2],), np.float32)
    col_scale[rb0:a10] = 0.5
    col_scale[a10 + GATE_RANK:] = 0.5
    w_main = (w_in * col_scale).astype(BF16)
    w_mg = w_main[:, :, a10 + GATE_RANK:]
    w_a2p = jnp.concatenate(
        [w_a2, jnp.zeros((depth, A1_PAD - GATE_RANK, kw), w_a2.dtype)], axis=1).astype(BF16)
    w_out_b, w_gate_b, w_up_b, w_down_b = (w.astype(BF16) for w in (0.5 * w_out, 0.5 * w_gate, w_up, w_down))
    gq2 = jnp.tile(g_q * (HEAD_DIM_A ** -0.5 * LOG2E), (1, LANES // HEAD_DIM_A))[:, None, :]
    gk2 = jnp.tile(g_k, (1, LANES // HEAD_DIM_A))[:, None, :]
    gob4 = g_ob[:, None, :]
    bias_p = _alibi_bias_t(CHUNK, WINDOW + CHUNK, WINDOW, (0, WINDOW, WINDOW - CHUNK))
    bias_s = _alibi_bias_t(ts, past + ts, past, (0,))
    sink_p = _sink_rows(sinks, CHUNK)
    sink_s = _sink_rows(sinks, ts)

    tq_p = _row_tile(tp, 512)
    tm_p = _row_tile(bp * tp, 512)
    tm_s = _row_tile(bs * ts, 512)

    xp = x_prompt.reshape(bp * tp, d)
    xs = x_sample.reshape(bs * ts, d)
    pk, pv, sk, sv = [], [], [], []
    ps = ss = None
    cache_k = cache_swa_k.reshape(depth, bs, past, kvw)
    cache_v = cache_swa_v.reshape(depth, bs, past, kvw)
    for l in range(depth):
        dense = dict(gob=gob4[l], wo=w_out_b, gf=g_ffn[l][None], wg=w_gate_b, wu=w_up_b, wd=w_down_b, l=l)
        attn, gla_in, gates = _inproj(xp, g_mix[l][None], gq2[l], gk2[l], w_main, w_mg, l, tm=tm_p)
        oa, kl, vl = _swa_prompt(attn, bias_p, sink_p[l], batch=bp, seq=tp, tq=tq_p)
        ob, ps = _gla(gla_in, w_a2p, b_a[l][None], None, ps, l, batch=bp, seq=tp, tq=tq_p, cs=CHUNK)
        xp = _merge_ffn(xp, oa, ob, gates, tm=tm_p, **dense)
        pk.append(kl.reshape(bp, WINDOW, N_KV_A, HEAD_DIM_A))
        pv.append(vl.reshape(bp, WINDOW, N_KV_A, HEAD_DIM_A))
        attn, gla_in, gates = _inproj(xs, g_mix[l][None], gq2[l], gk2[l], w_main, w_mg, l, tm=tm_s)
        oa, kn, vn = _swa_sample(attn, cache_k, cache_v, l, bias_s, sink_s[l], batch=bs, tq=ts, nb=_row_tile(bs, 4))
        ob, ss = _gla(gla_in, w_a2p, b_a[l][None], state_gla, ss, l, batch=bs, seq=ts, tq=ts, cs=ts)
        xs = _merge_ffn(xs, oa, ob, gates, tm=tm_s, **dense)
        sk.append(kn.reshape(bs, ts, N_KV_A, HEAD_DIM_A))
        sv.append(vn.reshape(bs, ts, N_KV_A, HEAD_DIM_A))

    return (xp.reshape(bp, tp, d), xs.reshape(bs, ts, d), jnp.stack(pk), jnp.stack(pv), ps,
            jnp.stack(sk), jnp.stack(sv), ss)
```

```python
import functools

import jax
import jax.numpy as jnp
import numpy as np
from jax import lax
from jax.experimental import pallas as pl
from jax.experimental.pallas import tpu as pltpu

CHUNK = 64
EPS = 1e-6
NEG = -1e30
N_HEADS_A = 16
N_KV_A = 4
GROUP_A = N_HEADS_A // N_KV_A
HEAD_DIM_A = 64
WINDOW = 128
N_HEADS_B = 4
DK_B = 128
DV_B = 256
GATE_RANK = 16
GATE_TAU = 16.0
GLA_BLOCK = 16

LANES = 128
A1_PAD = LANES
VMEM_LIMIT = 56 * 1024 * 1024
SCORE_RING = 8
PROB_RING = 6
PAD_ROWS = 16
DOWN_LAG = 2
LOG2E = 1.4426950408889634

BF16 = jnp.bfloat16
F32 = jnp.float32
NT = (((1,), (1,)), ((), ()))
TN = (((0,), (0,)), ((), ()))


def _cparams(*sem):
    return pltpu.CompilerParams(dimension_semantics=sem, vmem_limit_bytes=VMEM_LIMIT)


def _resident(shape):
    nd = len(shape)
    return pl.BlockSpec(shape, lambda *_: (0,) * nd, pipeline_mode=pl.Buffered(1))


def _layer(stacked, l):
    nd = stacked.ndim - 1
    return pl.BlockSpec((None,) + stacked.shape[1:], lambda *_: (l,) + (0,) * nd, pipeline_mode=pl.Buffered(1))


def _rms_rows(x, g):
    ms = jnp.mean(x * x, axis=-1, keepdims=True)
    return (x * lax.rsqrt(ms + EPS)) * g


def _head_norm_tile(x, g2, lo_half):
    sq = x * x
    ms_lo = jnp.sum(jnp.where(lo_half, sq, 0.0), axis=-1, keepdims=True) * (1.0 / HEAD_DIM_A)
    ms_hi = jnp.sum(jnp.where(lo_half, 0.0, sq), axis=-1, keepdims=True) * (1.0 / HEAD_DIM_A)
    r = jnp.where(lo_half, lax.rsqrt(ms_lo + EPS), lax.rsqrt(ms_hi + EPS))
    return (x * r) * g2


def _inproj_body(x_ref, g_ref, gq_ref, gk_ref, w_ref, oa_ref, ob_ref, oc_ref, *, plan):
    h = _rms_rows(x_ref[...], g_ref[...]).astype(BF16)
    lo_half = lax.broadcasted_iota(jnp.int32, (1, LANES), 1) < HEAD_DIM_A
    qw, kvw = N_HEADS_A * HEAD_DIM_A, N_KV_A * HEAD_DIM_A
    o_refs = (oa_ref, ob_ref, oc_ref)
    for c0, c1, dst, d0 in plan:
        acc = jnp.dot(h, w_ref[:, c0:c1], preferred_element_type=F32)
        if dst != 0:
            o_refs[dst][:, d0:d0 + (c1 - c0)] = acc.astype(BF16)
            continue
        for t0 in range(0, c1 - c0, LANES):
            col = d0 + t0
            tile = acc[:, t0:t0 + LANES]
            if col < qw:
                tile = _head_norm_tile(tile, gq_ref[...], lo_half)
            elif col < qw + kvw:
                tile = _head_norm_tile(tile, gk_ref[...], lo_half)
            oa_ref[:, col:col + LANES] = tile.astype(BF16)


def _inproj_plan(d, col_chunk):
    qw, kvw = N_HEADS_A * HEAD_DIM_A, N_KV_A * HEAD_DIM_A
    kw, vw = N_HEADS_B * DK_B, N_HEADS_B * DV_B
    attn_w, gla_w = qw + 2 * kvw, 2 * kw + vw
    a10 = attn_w + gla_w + vw
    runs = [(0, attn_w, 0, 0),
            (attn_w, attn_w + gla_w, 1, 0),
            (a10, a10 + A1_PAD, 1, gla_w),
            (attn_w + gla_w, a10, 2, 0),
            (a10 + GATE_RANK, a10 + GATE_RANK + 2 * d, 2, vw)]
    plan = []
    for c0, c1, dst, d0 in runs:
        for s0 in range(c0, c1, col_chunk):
            plan.append((s0, min(s0 + col_chunk, c1), dst, d0 + s0 - c0))
    return tuple(plan), (attn_w, gla_w + A1_PAD, vw + 2 * d)


def _inproj(x, g, gq2, gk2, w_main, l, *, tm):
    m, d = x.shape
    plan, widths = _inproj_plan(d, 512)
    return pl.pallas_call(
        functools.partial(_inproj_body, plan=plan),
        grid=(m // tm,),
        in_specs=[pl.BlockSpec((tm, d), lambda i: (i, 0)), _resident(g.shape),
                  _resident(gq2.shape), _resident(gk2.shape), _layer(w_main, l)],
        out_specs=[pl.BlockSpec((tm, w), lambda i: (i, 0)) for w in widths],
        out_shape=[jax.ShapeDtypeStruct((m, w), BF16) for w in widths],
        compiler_params=_cparams("parallel"),
        name="inproj",
    )(x, g, gq2, gk2, w_main)


def _dup_halves(x, lo_half):
    r = pltpu.roll(x, HEAD_DIM_A, axis=1)
    return jnp.where(lo_half, x, r), jnp.where(lo_half, r, x)


def _scores(lhs, kk, bias_t):
    return lax.dot_general(kk, lhs, NT, preferred_element_type=F32) + bias_t


def _sink_weights(s, sink_row):
    m = jnp.maximum(jnp.max(s, axis=0, keepdims=True), sink_row)
    p = jnp.exp2(s - m).astype(BF16)
    first = lax.broadcasted_iota(jnp.int32, (PAD_ROWS, s.shape[1]), 0) == 0
    tail = jnp.where(first, jnp.exp2(sink_row - m), 0.0).astype(BF16)
    return jnp.concatenate([p, tail], axis=0)


def _value_operand_tail(band):
    rows = band + PAD_ROWS
    counted = lax.broadcasted_iota(jnp.int32, (rows, LANES), 0) <= band
    return jnp.zeros((PAD_ROWS, LANES), BF16), jnp.where(counted, 1.0, 0.0).astype(BF16)


def _weighted_values(p, vv, tail, lo_half, cq):
    zero_rows, ones_blk = tail
    rhs = jnp.concatenate([jnp.concatenate([vv, zero_rows], axis=0), ones_blk], axis=1)
    o = lax.dot_general(p, rhs, TN, preferred_element_type=F32)
    num = jnp.where(lo_half, o[0:cq, 0:LANES], o[cq:2 * cq, 0:LANES])
    den = jnp.where(lo_half, o[0:cq, LANES:], o[cq:2 * cq, LANES:])
    return (num / den).astype(BF16)


def _half_masks():
    lane = lax.broadcasted_iota(jnp.int32, (1, LANES), 1)
    lo_half = lane < HEAD_DIM_A
    return lo_half, jnp.where(lo_half, 1.0, 0.0).astype(BF16), jnp.where(lo_half, 0.0, 1.0).astype(BF16)


def _swa_prompt_body(q_ref, kv_ref, bias_ref, sink_ref,
                     o_ref, klast_ref, vlast_ref, kk_s, vv_s, qm_s, s_buf, p_buf, *, tq):
    i = pl.program_id(1)
    kvw = N_KV_A * HEAD_DIM_A
    nchunk = tq // CHUNK
    lo_half, mask_lo, mask_hi = _half_masks()

    @pl.when(i == 0)
    def _():
        kk_s[:, 0:WINDOW, :] = jnp.zeros((N_KV_A, WINDOW, LANES), BF16)
        vv_s[:, 0:WINDOW, :] = jnp.zeros((N_KV_A, WINDOW, LANES), BF16)

    @pl.when(i > 0)
    def _():
        kk_s[:, 0:WINDOW, :] = kk_s[:, tq:tq + WINDOW, :]
        vv_s[:, 0:WINDOW, :] = vv_s[:, tq:tq + WINDOW, :]

    for t in range(kvw // LANES):
        k_t = kv_ref[:, t * LANES:(t + 1) * LANES].astype(F32)
        v_t = kv_ref[:, kvw + t * LANES:kvw + (t + 1) * LANES].astype(F32)
        klast_ref[0, :, t * LANES:(t + 1) * LANES] = k_t[tq - WINDOW:, :]
        vlast_ref[0, :, t * LANES:(t + 1) * LANES] = v_t[tq - WINDOW:, :]
        k0, k1 = _dup_halves(k_t, lo_half)
        v0, v1 = _dup_halves(v_t, lo_half)
        kk_s[2 * t, WINDOW:, :] = k0.astype(BF16)
        kk_s[2 * t + 1, WINDOW:, :] = k1.astype(BF16)
        vv_s[2 * t, WINDOW:, :] = v0.astype(BF16)
        vv_s[2 * t + 1, WINDOW:, :] = v1.astype(BF16)

    for t in range(N_HEADS_A * HEAD_DIM_A // LANES):
        q_t = q_ref[:, t * LANES:(t + 1) * LANES]
        qm_s[t, :, 0:CHUNK, :] = (q_t * mask_lo).reshape(nchunk, CHUNK, LANES)
        qm_s[t, :, CHUNK:, :] = (q_t * mask_hi).reshape(nchunk, CHUNK, LANES)

    band = WINDOW + CHUNK
    tail = _value_operand_tail(band)
    ntile = N_HEADS_A * HEAD_DIM_A // LANES
    nitem = nchunk * ntile
    n_sring, n_pring = s_buf.shape[0], p_buf.shape[0]
    skew_sm, skew_pv = n_sring - 2, n_pring - 1

    def keys(n):
        c, t = divmod(n, ntile)
        return t // 2, slice(c * CHUNK, c * CHUNK + band)

    def stage_scores(n):
        c, t = divmod(n, ntile)
        j, rows = keys(n)
        variant = jnp.where(i == 0, c + 1, 0) if c < WINDOW // CHUNK else 0
        s_buf[n % n_sring] = _scores(qm_s[t, c], kk_s[j, rows, :], bias_ref[variant, t])

    def stage_softmax(n):
        p_buf[n % n_pring] = _sink_weights(s_buf[n % n_sring], sink_ref[n % ntile])

    def stage_values(n):
        c, t = divmod(n, ntile)
        j, rows = keys(n)
        o_ref[c * CHUNK:(c + 1) * CHUNK, t * LANES:(t + 1) * LANES] = _weighted_values(
            p_buf[n % n_pring], vv_s[j, rows, :], tail, lo_half, CHUNK)

    for n in range(nitem + skew_sm + skew_pv):
        if n < nitem:
            stage_scores(n)
        if 0 <= n - skew_sm < nitem:
            stage_softmax(n - skew_sm)
        if 0 <= n - skew_sm - skew_pv < nitem:
            stage_values(n - skew_sm - skew_pv)


def _swa_prompt(attn, bias, sink_rows, *, batch, seq, tq):
    m = attn.shape[0]
    nt = seq // tq
    qw = N_HEADS_A * HEAD_DIM_A
    kvw = N_KV_A * HEAD_DIM_A
    row = lambda b, i: b * nt + i
    return pl.pallas_call(
        functools.partial(_swa_prompt_body, tq=tq),
        grid=(batch, nt),
        in_specs=[pl.BlockSpec((tq, qw), lambda b, i: (row(b, i), 0)),
                  pl.BlockSpec((tq, 2 * kvw), lambda b, i: (row(b, i), qw // (2 * kvw))),
                  _resident(bias.shape), _resident(sink_rows.shape)],
        out_specs=[pl.BlockSpec((tq, qw), lambda b, i: (row(b, i), 0)),
                   pl.BlockSpec((1, WINDOW, kvw), lambda b, i: (b, 0, 0)),
                   pl.BlockSpec((1, WINDOW, kvw), lambda b, i: (b, 0, 0))],
        out_shape=[jax.ShapeDtypeStruct((m, qw), BF16),
                   jax.ShapeDtypeStruct((batch, WINDOW, kvw), F32),
                   jax.ShapeDtypeStruct((batch, WINDOW, kvw), F32)],
        scratch_shapes=[pltpu.VMEM((N_KV_A, WINDOW + tq, LANES), BF16),
                        pltpu.VMEM((N_KV_A, WINDOW + tq, LANES), BF16),
                        pltpu.VMEM((qw // LANES, tq // CHUNK, 2 * CHUNK, LANES), BF16),
                        pltpu.VMEM((SCORE_RING, WINDOW + CHUNK, LANES), F32),
                        pltpu.VMEM((PROB_RING, WINDOW + CHUNK + PAD_ROWS, LANES), BF16)],
        compiler_params=_cparams("parallel", "arbitrary"),
        name="swa_prompt",
    )(attn, attn, bias, sink_rows)


def _swa_sample_body(q_ref, kv_ref, ck_ref, cv_ref, bias_ref, sink_ref,
                     o_ref, knew_ref, vnew_ref, *, tq, nb):
    kvw = N_KV_A * HEAD_DIM_A
    lo_half, mask_lo, mask_hi = _half_masks()
    tail = _value_operand_tail(ck_ref.shape[1] + tq)
    ntile = N_HEADS_A * HEAD_DIM_A // LANES
    vvs, scores = [], []
    for b in range(nb):
        rows = slice(b * tq, (b + 1) * tq)
        kk, vv = [None] * N_KV_A, [None] * N_KV_A
        for t in range(kvw // LANES):
            sl = slice(t * LANES, (t + 1) * LANES)
            k_t = kv_ref[rows, sl].astype(F32)
            v_t = kv_ref[rows, kvw + t * LANES:kvw + (t + 1) * LANES].astype(F32)
            knew_ref[b, :, sl] = k_t
            vnew_ref[b, :, sl] = v_t
            k0, k1 = _dup_halves(jnp.concatenate([ck_ref[b, :, sl], k_t], axis=0), lo_half)
            v0, v1 = _dup_halves(jnp.concatenate([cv_ref[b, :, sl], v_t], axis=0), lo_half)
            kk[2 * t], kk[2 * t + 1] = k0.astype(BF16), k1.astype(BF16)
            vv[2 * t], vv[2 * t + 1] = v0.astype(BF16), v1.astype(BF16)
        vvs.append(vv)
        for t in range(ntile):
            q_t = q_ref[rows, t * LANES:(t + 1) * LANES]
            lhs = jnp.concatenate([q_t * mask_lo, q_t * mask_hi], axis=0)
            scores.append(_scores(lhs, kk[t // 2], bias_ref[0, t]))
    weights = [_sink_weights(s, sink_ref[n % ntile]) for n, s in enumerate(scores)]
    for n, p in enumerate(weights):
        b, t = divmod(n, ntile)
        o_ref[b * tq:(b + 1) * tq, t * LANES:(t + 1) * LANES] = _weighted_values(
            p, vvs[b][t // 2], tail, lo_half, tq)


def _swa_sample(attn, cache_k, cache_v, l, bias, sink_rows, *, batch, tq, nb):
    m = attn.shape[0]
    qw = N_HEADS_A * HEAD_DIM_A
    kvw = N_KV_A * HEAD_DIM_A
    past = cache_k.shape[2]
    return pl.pallas_call(
        functools.partial(_swa_sample_body, tq=tq, nb=nb),
        grid=(batch // nb,),
        in_specs=[pl.BlockSpec((nb * tq, qw), lambda b: (b, 0)),
                  pl.BlockSpec((nb * tq, 2 * kvw), lambda b: (b, qw // (2 * kvw))),
                  pl.BlockSpec((None, nb, past, kvw), lambda b: (l, b, 0, 0)),
                  pl.BlockSpec((None, nb, past, kvw), lambda b: (l, b, 0, 0)),
                  _resident(bias.shape), _resident(sink_rows.shape)],
        out_specs=[pl.BlockSpec((nb * tq, qw), lambda b: (b, 0)),
                   pl.BlockSpec((nb, tq, kvw), lambda b: (b, 0, 0)),
                   pl.BlockSpec((nb, tq, kvw), lambda b: (b, 0, 0))],
        out_shape=[jax.ShapeDtypeStruct((m, qw), BF16),
                   jax.ShapeDtypeStruct((batch, tq, kvw), F32),
                   jax.ShapeDtypeStruct((batch, tq, kvw), F32)],
        compiler_params=_cparams("parallel"),
        name="swa_sample",
    )(attn, attn, cache_k, cache_v, bias, sink_rows)


def _log_sigmoid(z):
    return -(jnp.maximum(-z, 0.0) + jnp.log(1.0 + jnp.exp(-jnp.abs(z))))


def _gla_body(*refs, tq, cs, has_init, has_alias):
    refs = list(refs)
    qk_ref, v_ref, a1_ref, wa2_ref, ba_ref = refs[:5]
    s0_ref = refs[5] if has_init else None
    (o_ref, sout_ref, st_s, cum_s, qin_s, kout_s, qs_s, ks_s, dec_s, att_s, oi_s,
     upd_s) = refs[5 + has_init + has_alias:]
    i = pl.program_id(1)
    kw = N_HEADS_B * DK_B
    nchunk = tq // cs
    nsb = cs // GLA_BLOCK
    scale = DK_B ** -0.5
    chunk_rows = [slice(c * cs, (c + 1) * cs) for c in range(nchunk)]
    ksls = [slice(h * DK_B, (h + 1) * DK_B) for h in range(N_HEADS_B)]
    vsls = [slice(h * DV_B, (h + 1) * DV_B) for h in range(N_HEADS_B)]

    @pl.when(i == 0)
    def _():
        for h in range(N_HEADS_B):
            if has_init:
                st_s[h] = s0_ref[0, h].T
            else:
                st_s[h] = jnp.zeros((DV_B, DK_B), F32)

    z = jnp.dot(a1_ref[...], wa2_ref[...], preferred_element_type=F32) + ba_ref[...]
    la = _log_sigmoid(z) * (1.0 / GATE_TAU)
    row = lax.broadcasted_iota(jnp.int32, (cs, cs), 0)
    col = lax.broadcasted_iota(jnp.int32, (cs, cs), 1)
    causal = col <= row
    tri = jnp.where(causal, 1.0, 0.0).astype(BF16)
    la_hi = la.astype(BF16)
    la_lo = (la - la_hi.astype(F32)).astype(BF16)

    def stage1(c):
        rows = chunk_rows[c]
        cum_s[rows, :] = (jnp.dot(tri, la_hi[rows], preferred_element_type=F32)
                          + jnp.dot(tri, la_lo[rows], preferred_element_type=F32))

    def stage2(c):
        rows = chunk_rows[c]
        cum = cum_s[rows, :]
        q = qk_ref[rows, 0:kw]
        k = qk_ref[rows, kw:2 * kw]
        last = cum[cs - 1:cs, :]
        qin_s[rows, :] = q * (jnp.exp(cum) * scale).astype(BF16)
        kout_s[rows, :] = k * jnp.exp(last - cum).astype(BF16)
        dec_s[c] = jnp.exp(last)
        for sb in range(nsb):
            lo, hi = sb * GLA_BLOCK, (sb + 1) * GLA_BLOCK
            ref_row = cum[lo - 1:lo, :] if sb > 0 else jnp.zeros_like(last)
            qs_s[c * cs + lo:c * cs + hi, :] = q[lo:hi] * (jnp.exp(cum[lo:hi] - ref_row) * scale).astype(BF16)
            ks_s[sb, c * cs:c * cs + hi, :] = k[0:hi] * jnp.exp(ref_row - cum[0:hi]).astype(BF16)
            if hi < cs:
                ks_s[sb, c * cs + hi:(c + 1) * cs, :] = jnp.zeros((cs - hi, kw), BF16)

    def stage3a(c):
        rows = chunk_rows[c]
        for h in range(N_HEADS_B):
            att = jnp.concatenate(
                [lax.dot_general(qs_s[c * cs + sb * GLA_BLOCK:c * cs + (sb + 1) * GLA_BLOCK, ksls[h]],
                                 ks_s[sb, rows, ksls[h]], NT, preferred_element_type=F32) for sb in range(nsb)],
                axis=0)
            att_s[c, h] = jnp.where(causal, att, 0.0).astype(BF16)

    def stage3b(c):
        rows = chunk_rows[c]
        for h in range(N_HEADS_B):
            oi_s[rows, vsls[h]] = jnp.dot(att_s[c, h], v_ref[rows, vsls[h]], preferred_element_type=F32)
            upd_s[c, h] = lax.dot_general(v_ref[rows, vsls[h]], kout_s[rows, ksls[h]], TN,
                                          preferred_element_type=F32)

    def stage4(c):
        rows = chunk_rows[c]
        dec = dec_s[c]
        for h in range(N_HEADS_B):
            st = st_s[h]
            o = oi_s[rows, vsls[h]] + lax.dot_general(qin_s[rows, ksls[h]], st.astype(BF16), NT,
                                                      preferred_element_type=F32)
            o_ref[rows, vsls[h]] = o.astype(BF16)
            st_s[h] = st * dec[:, ksls[h]] + upd_s[c, h]

    stages = (stage1, stage2, stage3a, stage3b, stage4)
    for n in range(nchunk + len(stages) - 1):
        for depth, stage in enumerate(stages):
            if 0 <= n - depth < nchunk:
                stage(n - depth)

    @pl.when(i == pl.num_programs(1) - 1)
    def _():
        for h in range(N_HEADS_B):
            sout_ref[0, h] = st_s[h].T


def _gla(gla_in, wa2p, ba, s0, states, l, *, batch, seq, tq, cs):
    depth = wa2p.shape[0]
    m = gla_in.shape[0]
    nt = seq // tq
    kw = N_HEADS_B * DK_B
    vw = N_HEADS_B * DV_B
    row = lambda b, i: b * nt + i
    has_init = s0 is not None
    in_specs = [pl.BlockSpec((tq, 2 * kw), lambda b, i: (row(b, i), 0)),
                pl.BlockSpec((tq, vw), lambda b, i: (row(b, i), 2 * kw // vw)),
                pl.BlockSpec((tq, A1_PAD), lambda b, i: (row(b, i), (2 * kw + vw) // A1_PAD)),
                _layer(wa2p, l), _resident(ba.shape)]
    args = [gla_in, gla_in, gla_in, wa2p, ba]
    state_spec = pl.BlockSpec((None, 1, N_HEADS_B, DK_B, DV_B), lambda b, i: (l, b, 0, 0, 0))
    if has_init:
        in_specs.append(state_spec)
        args.append(s0)
    aliases = {}
    if states is not None:
        aliases[len(args)] = 1
        in_specs.append(pl.BlockSpec(memory_space=pl.ANY))
        args.append(states)
    nchunk = tq // cs
    return pl.pallas_call(
        functools.partial(_gla_body, tq=tq, cs=cs, has_init=has_init, has_alias=states is not None),
        grid=(batch, nt),
        in_specs=in_specs,
        out_specs=[pl.BlockSpec((tq, vw), lambda b, i: (row(b, i), 0)), state_spec],
        out_shape=[jax.ShapeDtypeStruct((m, vw), BF16),
                   jax.ShapeDtypeStruct((depth, batch, N_HEADS_B, DK_B, DV_B), F32)],
        input_output_aliases=aliases,
        scratch_shapes=[pltpu.VMEM((N_HEADS_B, DV_B, DK_B), F32),
                        pltpu.VMEM((tq, kw), F32),
                        pltpu.VMEM((tq, kw), BF16),
                        pltpu.VMEM((tq, kw), BF16),
                        pltpu.VMEM((tq, kw), BF16),
                        pltpu.VMEM((cs // GLA_BLOCK, tq, kw), BF16),
                        pltpu.VMEM((nchunk, 1, kw), F32),
                        pltpu.VMEM((nchunk, N_HEADS_B, cs, cs), BF16),
                        pltpu.VMEM((tq, vw), F32),
                        pltpu.VMEM((nchunk, N_HEADS_B, DV_B, DK_B), F32)],
        compiler_params=_cparams("parallel", "arbitrary"),
        name="gla_init" if has_init else "gla",
    )(*args)


def _merge_ffn_body(x_ref, oa_ref, ob_ref, rb_ref, ga_ref, gb_ref, gob_ref, wo_ref, gf_ref,
                    wg_ref, wu_ref, wd_ref, y_ref, *, ff_chunks, row_parts):
    tm = x_ref.shape[0]
    parts = [slice(k * tm // row_parts, (k + 1) * tm // row_parts) for k in range(row_parts)]
    gob = gob_ref[...]

    def mix2(rows):
        ob = ob_ref[rows, :].astype(F32)
        obn = jnp.concatenate(
            [_rms_rows(ob[:, h * DV_B:(h + 1) * DV_B], gob) for h in range(N_HEADS_B)], axis=1)
        rbh = rb_ref[rows, :].astype(F32)
        obg = (obn * (rbh * (jnp.tanh(rbh) + 1.0))).astype(BF16)
        one = jnp.ones((), BF16)
        return ((jnp.tanh(ga_ref[rows, :]) + one) * oa_ref[rows, :]
                + (jnp.tanh(gb_ref[rows, :]) + one) * obg)

    mixes = [mix2(rows) for rows in parts]
    acc = [x_ref[rows, :] + jnp.dot(mx, wo_ref[...], preferred_element_type=F32)
           for rows, mx in zip(parts, mixes)]
    hs = [_rms_rows(x1, gf_ref[...]).astype(BF16) for x1 in acc]
    units = [(c0, c1, k) for c0, c1 in ff_chunks for k in range(row_parts)]
    act = {}
    for n in range(len(units) + DOWN_LAG):
        if n < len(units):
            c0, c1, k = units[n]
            gh = jnp.dot(hs[k], wg_ref[:, c0:c1], preferred_element_type=F32)
            u = jnp.dot(hs[k], wu_ref[:, c0:c1], preferred_element_type=F32)
            act[n] = ((gh * (jnp.tanh(gh) + 1.0)) * u).astype(BF16)
        if n >= DOWN_LAG:
            c0, c1, k = units[n - DOWN_LAG]
            acc[k] = acc[k] + jnp.dot(act.pop(n - DOWN_LAG), wd_ref[c0:c1, :], preferred_element_type=F32)
    for rows, y in zip(parts, acc):
        y_ref[rows, :] = y


def _ff_chunks(f, pref):
    mxu = 2 * LANES
    step = pref if f % mxu == 0 else f
    edges = list(range(0, f, step)) + [f]
    return tuple(zip(edges[:-1], edges[1:]))


def _merge_ffn(x, oa, ob, gates, gob, wo, gf, wg, wu, wd, l, *, tm):
    m, d = x.shape
    tile = lambda c: pl.BlockSpec((tm, d), lambda i: (i, c))
    return pl.pallas_call(
        functools.partial(_merge_ffn_body, ff_chunks=_ff_chunks(wg.shape[2], 1024),
                          row_parts=2 if tm % 32 == 0 else 1),
        grid=(m // tm,),
        in_specs=[tile(0), tile(0), tile(0), tile(0), tile(1), tile(2),
                  _resident(gob.shape), _layer(wo, l), _resident(gf.shape),
                  _layer(wg, l), _layer(wu, l), _layer(wd, l)],
        out_specs=tile(0),
        out_shape=jax.ShapeDtypeStruct((m, d), F32),
        compiler_params=_cparams("parallel"),
        name="merge_ffn",
    )(x, oa, ob, gates, gates, gates, gob, wo, gf, wg, wu, wd)


def _alibi_bias_t(cq, band, q_off, masked_prefixes):
    slopes = LOG2E * 2.0 ** (-8.0 * np.arange(1, N_HEADS_A + 1, dtype=np.float64) / N_HEADS_A)
    dist = np.abs(np.arange(band)[:, None] - q_off - np.arange(cq)[None, :])
    bias = -slopes.reshape(N_HEADS_A // 2, 1, 2, 1) * dist[None, :, None, :]
    bias = bias.reshape(N_HEADS_A // 2, band, 2 * cq)
    out = np.stack([np.where(np.arange(band)[None, :, None] < p, NEG, bias) for p in masked_prefixes])
    return jnp.asarray(out, F32)


def _sink_rows(sinks, cq):
    depth = sinks.shape[0]
    return jnp.repeat(sinks.reshape(depth, N_HEADS_A // 2, 2) * LOG2E, cq, axis=2)[:, :, None, :]


def _row_tile(m, pref):
    t = min(pref, m)
    while m % t:
        t //= 2
    return t


def kernel(x_prompt, x_sample, cache_swa_k, cache_swa_v, state_gla, w_in, w_a2, b_a, g_q, g_k, sinks,
           g_ob, w_out, g_mix, g_ffn, w_gate, w_up, w_down):
    depth = w_in.shape[0]
    bp, tp, d = x_prompt.shape
    bs, ts, _ = x_sample.shape
    past = cache_swa_k.shape[2]
    qw, kvw = N_HEADS_A * HEAD_DIM_A, N_KV_A * HEAD_DIM_A
    kw, vw = N_HEADS_B * DK_B, N_HEADS_B * DV_B
    assert past == WINDOW and tp % CHUNK == 0 and ts % GLA_BLOCK == 0 and ts <= CHUNK

    rb0 = qw + 2 * kvw + 2 * kw + vw
    a10 = rb0 + vw
    col_scale = np.ones((w_in.shape[2],), np.float32)
    col_scale[rb0:a10] = 0.5
    col_scale[a10 + GATE_RANK:] = 0.5
    w_main = (w_in * col_scale).astype(BF16)
    w_a2p = jnp.concatenate(
        [w_a2, jnp.zeros((depth, A1_PAD - GATE_RANK, kw), w_a2.dtype)], axis=1).astype(BF16)
    w_out_b, w_gate_b, w_up_b, w_down_b = (w.astype(BF16) for w in (0.5 * w_out, 0.5 * w_gate, w_up, w_down))
    gq2 = jnp.tile(g_q * (HEAD_DIM_A ** -0.5 * LOG2E), (1, LANES // HEAD_DIM_A))[:, None, :]
    gk2 = jnp.tile(g_k, (1, LANES // HEAD_DIM_A))[:, None, :]
    gob4 = g_ob[:, None, :]
    bias_p = _alibi_bias_t(CHUNK, WINDOW + CHUNK, WINDOW, (0, WINDOW, WINDOW - CHUNK))
    bias_s = _alibi_bias_t(ts, past + ts, past, (0,))
    sink_p = _sink_rows(sinks, CHUNK)
    sink_s = _sink_rows(sinks, ts)

    tq_p = _row_tile(tp, 1024)
    tm_p = _row_tile(bp * tp, 512)
    tm_s = _row_tile(bs * ts, 512)

    xp = x_prompt.reshape(bp * tp, d)
    xs = x_sample.reshape(bs * ts, d)
    pk, pv, sk, sv = [], [], [], []
    ps = ss = None
    cache_k = cache_swa_k.reshape(depth, bs, past, kvw)
    cache_v = cache_swa_v.reshape(depth, bs, past, kvw)
    for l in range(depth):
        dense = dict(gob=gob4[l], wo=w_out_b, gf=g_ffn[l][None], wg=w_gate_b, wu=w_up_b, wd=w_down_b, l=l)
        attn, gla_in, gates = _inproj(xp, g_mix[l][None], gq2[l], gk2[l], w_main, l, tm=tm_p)
        oa, kl, vl = _swa_prompt(attn, bias_p, sink_p[l], batch=bp, seq=tp, tq=tq_p)
        ob, ps = _gla(gla_in, w_a2p, b_a[l][None], None, ps, l, batch=bp, seq=tp, tq=tq_p, cs=CHUNK)
        xp = _merge_ffn(xp, oa, ob, gates, tm=tm_p, **dense)
        pk.append(kl.reshape(bp, WINDOW, N_KV_A, HEAD_DIM_A))
        pv.append(vl.reshape(bp, WINDOW, N_KV_A, HEAD_DIM_A))
        attn, gla_in, gates = _inproj(xs, g_mix[l][None], gq2[l], gk2[l], w_main, l, tm=tm_s)
        oa, kn, vn = _swa_sample(attn, cache_k, cache_v, l, bias_s, sink_s[l], batch=bs, tq=ts, nb=_row_tile(bs, 4))
        ob, ss = _gla(gla_in, w_a2p, b_a[l][None], state_gla, ss, l, batch=bs, seq=ts, tq=ts, cs=ts)
        xs = _merge_ffn(xs, oa, ob, gates, tm=tm_s, **dense)
        sk.append(kn.reshape(bs, ts, N_KV_A, HEAD_DIM_A))
        sv.append(vn.reshape(bs, ts, N_KV_A, HEAD_DIM_A))

    return (xp.reshape(bp, tp, d), xs.reshape(bs, ts, d), jnp.stack(pk), jnp.stack(pv), ps,
            jnp.stack(sk), jnp.stack(sv), ss)
```

```python
import functools

import jax
import jax.numpy as jnp
import numpy as np
from jax import lax
from jax.experimental import pallas as pl
from jax.experimental.pallas import tpu as pltpu

CHUNK = 64
EPS = 1e-6
NEG = -1e30
N_HEADS_A = 16
N_KV_A = 4
GROUP_A = N_HEADS_A // N_KV_A
HEAD_DIM_A = 64
WINDOW = 128
N_HEADS_B = 4
DK_B = 128
DV_B = 256
GATE_RANK = 16
GATE_TAU = 16.0
GLA_BLOCK = 16

LANES = 128
A1_PAD = LANES
VMEM_LIMIT = 56 * 1024 * 1024
SCORE_RING = 8
PROB_RING = 6
PAD_ROWS = 16
DOWN_LAG = 2
LOG2E = 1.4426950408889634

BF16 = jnp.bfloat16
F32 = jnp.float32
NT = (((1,), (1,)), ((), ()))
TN = (((0,), (0,)), ((), ()))


def _cparams(*sem):
    return pltpu.CompilerParams(dimension_semantics=sem, vmem_limit_bytes=VMEM_LIMIT)


def _resident(shape):
    nd = len(shape)
    return pl.BlockSpec(shape, lambda *_: (0,) * nd, pipeline_mode=pl.Buffered(1))


def _layer(stacked, l):
    nd = stacked.ndim - 1
    return pl.BlockSpec((None,) + stacked.shape[1:], lambda *_: (l,) + (0,) * nd, pipeline_mode=pl.Buffered(1))


def _rms_rows(x, g):
    ms = jnp.mean(x * x, axis=-1, keepdims=True)
    return (x * lax.rsqrt(ms + EPS)) * g


def _scale_cast_body(w_ref, s_ref, o_ref):
    n = w_ref.shape[1]
    o_ref[:, 0:n] = (w_ref[...] * s_ref[...]).astype(BF16)
    if o_ref.shape[1] > n:
        o_ref[:, n:] = jnp.zeros((o_ref.shape[0], o_ref.shape[1] - n), BF16)


def _scale_cast(w, col_scale, *, tr):
    depth, rows, n = w.shape
    n_pad = n + (-n % LANES)
    return pl.pallas_call(
        _scale_cast_body,
        grid=(depth, rows // tr),
        in_specs=[pl.BlockSpec((None, tr, n), lambda l, i: (l, i, 0)), _resident(col_scale.shape)],
        out_specs=pl.BlockSpec((None, tr, n_pad), lambda l, i: (l, i, 0)),
        out_shape=jax.ShapeDtypeStruct((depth, rows, n_pad), BF16),
        compiler_params=_cparams("parallel", "parallel"),
        name="scale_cast",
    )(w, col_scale)


def _head_norm_tile(x, g2, lo_half):
    sq = x * x
    ms_lo = jnp.sum(jnp.where(lo_half, sq, 0.0), axis=-1, keepdims=True) * (1.0 / HEAD_DIM_A)
    ms_hi = jnp.sum(jnp.where(lo_half, 0.0, sq), axis=-1, keepdims=True) * (1.0 / HEAD_DIM_A)
    r = jnp.where(lo_half, lax.rsqrt(ms_lo + EPS), lax.rsqrt(ms_hi + EPS))
    return (x * r) * g2


def _inproj_body(x_ref, g_ref, gq_ref, gk_ref, w_ref, oa_ref, ob_ref, oc_ref, *, plan):
    h = _rms_rows(x_ref[...], g_ref[...]).astype(BF16)
    lo_half = lax.broadcasted_iota(jnp.int32, (1, LANES), 1) < HEAD_DIM_A
    qw, kvw = N_HEADS_A * HEAD_DIM_A, N_KV_A * HEAD_DIM_A
    o_refs = (oa_ref, ob_ref, oc_ref)
    for c0, c1, dst, d0 in plan:
        acc = jnp.dot(h, w_ref[:, c0:c1], preferred_element_type=F32)
        if dst != 0:
            o_refs[dst][:, d0:d0 + (c1 - c0)] = acc.astype(BF16)
            continue
        for t0 in range(0, c1 - c0, LANES):
            col = d0 + t0
            tile = acc[:, t0:t0 + LANES]
            if col < qw:
                tile = _head_norm_tile(tile, gq_ref[...], lo_half)
            elif col < qw + kvw:
                tile = _head_norm_tile(tile, gk_ref[...], lo_half)
            oa_ref[:, col:col + LANES] = tile.astype(BF16)


def _inproj_plan(d, col_chunk):
    qw, kvw = N_HEADS_A * HEAD_DIM_A, N_KV_A * HEAD_DIM_A
    kw, vw = N_HEADS_B * DK_B, N_HEADS_B * DV_B
    attn_w, gla_w = qw + 2 * kvw, 2 * kw + vw
    a10 = attn_w + gla_w + vw
    runs = [(0, attn_w, 0, 0),
            (attn_w, attn_w + gla_w, 1, 0),
            (a10, a10 + A1_PAD, 1, gla_w),
            (attn_w + gla_w, a10, 2, 0),
            (a10 + GATE_RANK, a10 + GATE_RANK + 2 * d, 2, vw)]
    plan = []
    for c0, c1, dst, d0 in runs:
        for s0 in range(c0, c1, col_chunk):
            plan.append((s0, min(s0 + col_chunk, c1), dst, d0 + s0 - c0))
    return tuple(plan), (attn_w, gla_w + A1_PAD, vw + 2 * d)


def _inproj(x, g, gq2, gk2, w_main, l, *, tm):
    m, d = x.shape
    plan, widths = _inproj_plan(d, 512)
    return pl.pallas_call(
        functools.partial(_inproj_body, plan=plan),
        grid=(m // tm,),
        in_specs=[pl.BlockSpec((tm, d), lambda i: (i, 0)), _resident(g.shape),
                  _resident(gq2.shape), _resident(gk2.shape), _layer(w_main, l)],
        out_specs=[pl.BlockSpec((tm, w), lambda i: (i, 0)) for w in widths],
        out_shape=[jax.ShapeDtypeStruct((m, w), BF16) for w in widths],
        compiler_params=_cparams("parallel"),
        name="inproj",
    )(x, g, gq2, gk2, w_main)


def _dup_halves(x, lo_half):
    r = pltpu.roll(x, HEAD_DIM_A, axis=1)
    return jnp.where(lo_half, x, r), jnp.where(lo_half, r, x)


def _scores(lhs, kk, bias_t):
    return lax.dot_general(kk, lhs, NT, preferred_element_type=F32) + bias_t


def _sink_weights(s, sink_row):
    m = jnp.maximum(jnp.max(s, axis=0, keepdims=True), sink_row)
    p = jnp.exp2(s - m).astype(BF16)
    first = lax.broadcasted_iota(jnp.int32, (PAD_ROWS, s.shape[1]), 0) == 0
    tail = jnp.where(first, jnp.exp2(sink_row - m), 0.0).astype(BF16)
    return jnp.concatenate([p, tail], axis=0)


def _value_operand_tail(band):
    rows = band + PAD_ROWS
    counted = lax.broadcasted_iota(jnp.int32, (rows, LANES), 0) <= band
    return jnp.zeros((PAD_ROWS, LANES), BF16), jnp.where(counted, 1.0, 0.0).astype(BF16)


def _weighted_values(p, vv, tail, lo_half, cq):
    zero_rows, ones_blk = tail
    rhs = jnp.concatenate([jnp.concatenate([vv, zero_rows], axis=0), ones_blk], axis=1)
    o = lax.dot_general(p, rhs, TN, preferred_element_type=F32)
    num = jnp.where(lo_half, o[0:cq, 0:LANES], o[cq:2 * cq, 0:LANES])
    den = jnp.where(lo_half, o[0:cq, LANES:], o[cq:2 * cq, LANES:])
    return (num / den).astype(BF16)


def _half_masks():
    lane = lax.broadcasted_iota(jnp.int32, (1, LANES), 1)
    lo_half = lane < HEAD_DIM_A
    return lo_half, jnp.where(lo_half, 1.0, 0.0).astype(BF16), jnp.where(lo_half, 0.0, 1.0).astype(BF16)


def _swa_prompt_body(q_ref, kv_ref, bias_ref, sink_ref,
                     o_ref, klast_ref, vlast_ref, kk_s, vv_s, qm_s, s_buf, p_buf, *, tq):
    i = pl.program_id(1)
    kvw = N_KV_A * HEAD_DIM_A
    nchunk = tq // CHUNK
    lo_half, mask_lo, mask_hi = _half_masks()

    @pl.when(i == 0)
    def _():
        kk_s[:, 0:WINDOW, :] = jnp.zeros((N_KV_A, WINDOW, LANES), BF16)
        vv_s[:, 0:WINDOW, :] = jnp.zeros((N_KV_A, WINDOW, LANES), BF16)

    @pl.when(i > 0)
    def _():
        kk_s[:, 0:WINDOW, :] = kk_s[:, tq:tq + WINDOW, :]
        vv_s[:, 0:WINDOW, :] = vv_s[:, tq:tq + WINDOW, :]

    for t in range(kvw // LANES):
        k_t = kv_ref[:, t * LANES:(t + 1) * LANES].astype(F32)
        v_t = kv_ref[:, kvw + t * LANES:kvw + (t + 1) * LANES].astype(F32)
        klast_ref[0, :, t * LANES:(t + 1) * LANES] = k_t[tq - WINDOW:, :]
        vlast_ref[0, :, t * LANES:(t + 1) * LANES] = v_t[tq - WINDOW:, :]
        k0, k1 = _dup_halves(k_t, lo_half)
        v0, v1 = _dup_halves(v_t, lo_half)
        kk_s[2 * t, WINDOW:, :] = k0.astype(BF16)
        kk_s[2 * t + 1, WINDOW:, :] = k1.astype(BF16)
        vv_s[2 * t, WINDOW:, :] = v0.astype(BF16)
        vv_s[2 * t + 1, WINDOW:, :] = v1.astype(BF16)

    for t in range(N_HEADS_A * HEAD_DIM_A // LANES):
        q_t = q_ref[:, t * LANES:(t + 1) * LANES]
        qm_s[t, :, 0:CHUNK, :] = (q_t * mask_lo).reshape(nchunk, CHUNK, LANES)
        qm_s[t, :, CHUNK:, :] = (q_t * mask_hi).reshape(nchunk, CHUNK, LANES)

    band = WINDOW + CHUNK
    tail = _value_operand_tail(band)
    ntile = N_HEADS_A * HEAD_DIM_A // LANES
    nitem = nchunk * ntile
    n_sring, n_pring = s_buf.shape[0], p_buf.shape[0]
    skew_sm, skew_pv = n_sring - 2, n_pring - 1

    def keys(n):
        c, t = divmod(n, ntile)
        return t // 2, slice(c * CHUNK, c * CHUNK + band)

    def stage_scores(n):
        c, t = divmod(n, ntile)
        j, rows = keys(n)
        variant = jnp.where(i == 0, c + 1, 0) if c < WINDOW // CHUNK else 0
        s_buf[n % n_sring] = _scores(qm_s[t, c], kk_s[j, rows, :], bias_ref[variant, t])

    def stage_softmax(n):
        p_buf[n % n_pring] = _sink_weights(s_buf[n % n_sring], sink_ref[n % ntile])

    def stage_values(n):
        c, t = divmod(n, ntile)
        j, rows = keys(n)
        o_ref[c * CHUNK:(c + 1) * CHUNK, t * LANES:(t + 1) * LANES] = _weighted_values(
            p_buf[n % n_pring], vv_s[j, rows, :], tail, lo_half, CHUNK)

    for n in range(nitem + skew_sm + skew_pv):
        if n < nitem:
            stage_scores(n)
        if 0 <= n - skew_sm < nitem:
            stage_softmax(n - skew_sm)
        if 0 <= n - skew_sm - skew_pv < nitem:
            stage_values(n - skew_sm - skew_pv)


def _swa_prompt(attn, bias, sink_rows, *, batch, seq, tq):
    m = attn.shape[0]
    nt = seq // tq
    qw = N_HEADS_A * HEAD_DIM_A
    kvw = N_KV_A * HEAD_DIM_A
    row = lambda b, i: b * nt + i
    return pl.pallas_call(
        functools.partial(_swa_prompt_body, tq=tq),
        grid=(batch, nt),
        in_specs=[pl.BlockSpec((tq, qw), lambda b, i: (row(b, i), 0)),
                  pl.BlockSpec((tq, 2 * kvw), lambda b, i: (row(b, i), qw // (2 * kvw))),
                  _resident(bias.shape), _resident(sink_rows.shape)],
        out_specs=[pl.BlockSpec((tq, qw), lambda b, i: (row(b, i), 0)),
                   pl.BlockSpec((1, WINDOW, kvw), lambda b, i: (b, 0, 0)),
                   pl.BlockSpec((1, WINDOW, kvw), lambda b, i: (b, 0, 0))],
        out_shape=[jax.ShapeDtypeStruct((m, qw), BF16),
                   jax.ShapeDtypeStruct((batch, WINDOW, kvw), F32),
                   jax.ShapeDtypeStruct((batch, WINDOW, kvw), F32)],
        scratch_shapes=[pltpu.VMEM((N_KV_A, WINDOW + tq, LANES), BF16),
                        pltpu.VMEM((N_KV_A, WINDOW + tq, LANES), BF16),
                        pltpu.VMEM((qw // LANES, tq // CHUNK, 2 * CHUNK, LANES), BF16),
                        pltpu.VMEM((SCORE_RING, WINDOW + CHUNK, LANES), F32),
                        pltpu.VMEM((PROB_RING, WINDOW + CHUNK + PAD_ROWS, LANES), BF16)],
        compiler_params=_cparams("parallel", "arbitrary"),
        name="swa_prompt",
    )(attn, attn, bias, sink_rows)


def _swa_sample_body(q_ref, kv_ref, ck_ref, cv_ref, bias_ref, sink_ref,
                     o_ref, knew_ref, vnew_ref, *, tq, nb):
    kvw = N_KV_A * HEAD_DIM_A
    lo_half, mask_lo, mask_hi = _half_masks()
    tail = _value_operand_tail(ck_ref.shape[1] + tq)
    ntile = N_HEADS_A * HEAD_DIM_A // LANES
    vvs, scores = [], []
    for b in range(nb):
        rows = slice(b * tq, (b + 1) * tq)
        kk, vv = [None] * N_KV_A, [None] * N_KV_A
        for t in range(kvw // LANES):
            sl = slice(t * LANES, (t + 1) * LANES)
            k_t = kv_ref[rows, sl].astype(F32)
            v_t = kv_ref[rows, kvw + t * LANES:kvw + (t + 1) * LANES].astype(F32)
            knew_ref[b, :, sl] = k_t
            vnew_ref[b, :, sl] = v_t
            k0, k1 = _dup_halves(jnp.concatenate([ck_ref[b, :, sl], k_t], axis=0), lo_half)
            v0, v1 = _dup_halves(jnp.concatenate([cv_ref[b, :, sl], v_t], axis=0), lo_half)
            kk[2 * t], kk[2 * t + 1] = k0.astype(BF16), k1.astype(BF16)
            vv[2 * t], vv[2 * t + 1] = v0.astype(BF16), v1.astype(BF16)
        vvs.append(vv)
        for t in range(ntile):
            q_t = q_ref[rows, t * LANES:(t + 1) * LANES]
            lhs = jnp.concatenate([q_t * mask_lo, q_t * mask_hi], axis=0)
            scores.append(_scores(lhs, kk[t // 2], bias_ref[0, t]))
    weights = [_sink_weights(s, sink_ref[n % ntile]) for n, s in enumerate(scores)]
    for n, p in enumerate(weights):
        b, t = divmod(n, ntile)
        o_ref[b * tq:(b + 1) * tq, t * LANES:(t + 1) * LANES] = _weighted_values(
            p, vvs[b][t // 2], tail, lo_half, tq)


def _swa_sample(attn, cache_k, cache_v, l, bias, sink_rows, *, batch, tq, nb):
    m = attn.shape[0]
    qw = N_HEADS_A * HEAD_DIM_A
    kvw = N_KV_A * HEAD_DIM_A
    past = cache_k.shape[2]
    return pl.pallas_call(
        functools.partial(_swa_sample_body, tq=tq, nb=nb),
        grid=(batch // nb,),
        in_specs=[pl.BlockSpec((nb * tq, qw), lambda b: (b, 0)),
                  pl.BlockSpec((nb * tq, 2 * kvw), lambda b: (b, qw // (2 * kvw))),
                  pl.BlockSpec((None, nb, past, kvw), lambda b: (l, b, 0, 0)),
                  pl.BlockSpec((None, nb, past, kvw), lambda b: (l, b, 0, 0)),
                  _resident(bias.shape), _resident(sink_rows.shape)],
        out_specs=[pl.BlockSpec((nb * tq, qw), lambda b: (b, 0)),
                   pl.BlockSpec((nb, tq, kvw), lambda b: (b, 0, 0)),
                   pl.BlockSpec((nb, tq, kvw), lambda b: (b, 0, 0))],
        out_shape=[jax.ShapeDtypeStruct((m, qw), BF16),
                   jax.ShapeDtypeStruct((batch, tq, kvw), F32),
                   jax.ShapeDtypeStruct((batch, tq, kvw), F32)],
        compiler_params=_cparams("parallel"),
        name="swa_sample",
    )(attn, attn, cache_k, cache_v, bias, sink_rows)


def _log_sigmoid(z):
    return -(jnp.maximum(-z, 0.0) + jnp.log(1.0 + jnp.exp(-jnp.abs(z))))


def _gla_body(*refs, tq, cs, has_init, has_alias):
    refs = list(refs)
    qk_ref, v_ref, a1_ref, wa2_ref, ba_ref = refs[:5]
    s0_ref = refs[5] if has_init else None
    (o_ref, sout_ref, st_s, cum_s, qin_s, kout_s, qs_s, ks_s, dec_s, att_s, oi_s,
     upd_s) = refs[5 + has_init + has_alias:]
    i = pl.program_id(1)
    kw = N_HEADS_B * DK_B
    nchunk = tq // cs
    nsb = cs // GLA_BLOCK
    scale = DK_B ** -0.5
    chunk_rows = [slice(c * cs, (c + 1) * cs) for c in range(nchunk)]
    ksls = [slice(h * DK_B, (h + 1) * DK_B) for h in range(N_HEADS_B)]
    vsls = [slice(h * DV_B, (h + 1) * DV_B) for h in range(N_HEADS_B)]

    @pl.when(i == 0)
    def _():
        for h in range(N_HEADS_B):
            if has_init:
                st_s[h] = s0_ref[0, h].T
            else:
                st_s[h] = jnp.zeros((DV_B, DK_B), F32)

    z = jnp.dot(a1_ref[...], wa2_ref[...], preferred_element_type=F32) + ba_ref[...]
    la = _log_sigmoid(z) * (1.0 / GATE_TAU)
    row = lax.broadcasted_iota(jnp.int32, (cs, cs), 0)
    col = lax.broadcasted_iota(jnp.int32, (cs, cs), 1)
    causal = col <= row
    tri = jnp.where(causal, 1.0, 0.0).astype(BF16)
    la_hi = la.astype(BF16)
    la_lo = (la - la_hi.astype(F32)).astype(BF16)

    def stage1(c):
        rows = chunk_rows[c]
        cum_s[rows, :] = (jnp.dot(tri, la_hi[rows], preferred_element_type=F32)
                          + jnp.dot(tri, la_lo[rows], preferred_element_type=F32))

    def stage2(c):
        rows = chunk_rows[c]
        cum = cum_s[rows, :]
        q = qk_ref[rows, 0:kw]
        k = qk_ref[rows, kw:2 * kw]
        last = cum[cs - 1:cs, :]
        qin_s[rows, :] = q * (jnp.exp(cum) * scale).astype(BF16)
        kout_s[rows, :] = k * jnp.exp(last - cum).astype(BF16)
        dec_s[c] = jnp.exp(last)
        for sb in range(nsb):
            lo, hi = sb * GLA_BLOCK, (sb + 1) * GLA_BLOCK
            ref_row = cum[lo - 1:lo, :] if sb > 0 else jnp.zeros_like(last)
            qs_s[c * cs + lo:c * cs + hi, :] = q[lo:hi] * (jnp.exp(cum[lo:hi] - ref_row) * scale).astype(BF16)
            ks_s[sb, c * cs:c * cs + hi, :] = k[0:hi] * jnp.exp(ref_row - cum[0:hi]).astype(BF16)
            if hi < cs:
                ks_s[sb, c * cs + hi:(c + 1) * cs, :] = jnp.zeros((cs - hi, kw), BF16)

    def stage3a(c):
        rows = chunk_rows[c]
        for h in range(N_HEADS_B):
            att = jnp.concatenate(
                [lax.dot_general(qs_s[c * cs + sb * GLA_BLOCK:c * cs + (sb + 1) * GLA_BLOCK, ksls[h]],
                                 ks_s[sb, rows, ksls[h]], NT, preferred_element_type=F32) for sb in range(nsb)],
                axis=0)
            att_s[c, h] = jnp.where(causal, att, 0.0).astype(BF16)

    def stage3b(c):
        rows = chunk_rows[c]
        for h in range(N_HEADS_B):
            oi_s[rows, vsls[h]] = jnp.dot(att_s[c, h], v_ref[rows, vsls[h]], preferred_element_type=F32)
            upd_s[c, h] = lax.dot_general(v_ref[rows, vsls[h]], kout_s[rows, ksls[h]], TN,
                                          preferred_element_type=F32)

    def stage4(c):
        rows = chunk_rows[c]
        dec = dec_s[c]
        for h in range(N_HEADS_B):
            st = st_s[h]
            o = oi_s[rows, vsls[h]] + lax.dot_general(qin_s[rows, ksls[h]], st.astype(BF16), NT,
                                                      preferred_element_type=F32)
            o_ref[rows, vsls[h]] = o.astype(BF16)
            st_s[h] = st * dec[:, ksls[h]] + upd_s[c, h]

    stages = (stage1, stage2, stage3a, stage3b, stage4)
    for n in range(nchunk + len(stages) - 1):
        for depth, stage in enumerate(stages):
            if 0 <= n - depth < nchunk:
                stage(n - depth)

    @pl.when(i == pl.num_programs(1) - 1)
    def _():
        for h in range(N_HEADS_B):
            sout_ref[0, h] = st_s[h].T


def _gla(gla_in, wa2p, ba, s0, states, l, *, batch, seq, tq, cs):
    depth = wa2p.shape[0]
    m = gla_in.shape[0]
    nt = seq // tq
    kw = N_HEADS_B * DK_B
    vw = N_HEADS_B * DV_B
    row = lambda b, i: b * nt + i
    has_init = s0 is not None
    in_specs = [pl.BlockSpec((tq, 2 * kw), lambda b, i: (row(b, i), 0)),
                pl.BlockSpec((tq, vw), lambda b, i: (row(b, i), 2 * kw // vw)),
                pl.BlockSpec((tq, A1_PAD), lambda b, i: (row(b, i), (2 * kw + vw) // A1_PAD)),
                _layer(wa2p, l), _resident(ba.shape)]
    args = [gla_in, gla_in, gla_in, wa2p, ba]
    state_spec = pl.BlockSpec((None, 1, N_HEADS_B, DK_B, DV_B), lambda b, i: (l, b, 0, 0, 0))
    if has_init:
        in_specs.append(state_spec)
        args.append(s0)
    aliases = {}
    if states is not None:
        aliases[len(args)] = 1
        in_specs.append(pl.BlockSpec(memory_space=pl.ANY))
        args.append(states)
    nchunk = tq // cs
    return pl.pallas_call(
        functools.partial(_gla_body, tq=tq, cs=cs, has_init=has_init, has_alias=states is not None),
        grid=(batch, nt),
        in_specs=in_specs,
        out_specs=[pl.BlockSpec((tq, vw), lambda b, i: (row(b, i), 0)), state_spec],
        out_shape=[jax.ShapeDtypeStruct((m, vw), BF16),
                   jax.ShapeDtypeStruct((depth, batch, N_HEADS_B, DK_B, DV_B), F32)],
        input_output_aliases=aliases,
        scratch_shapes=[pltpu.VMEM((N_HEADS_B, DV_B, DK_B), F32),
                        pltpu.VMEM((tq, kw), F32),
                        pltpu.VMEM((tq, kw), BF16),
                        pltpu.VMEM((tq, kw), BF16),
                        pltpu.VMEM((tq, kw), BF16),
                        pltpu.VMEM((cs // GLA_BLOCK, tq, kw), BF16),
                        pltpu.VMEM((nchunk, 1, kw), F32),
                        pltpu.VMEM((nchunk, N_HEADS_B, cs, cs), BF16),
                        pltpu.VMEM((tq, vw), F32),
                        pltpu.VMEM((nchunk, N_HEADS_B, DV_B, DK_B), F32)],
        compiler_params=_cparams("parallel", "arbitrary"),
        name="gla_init" if has_init else "gla",
    )(*args)


def _merge_ffn_body(x_ref, oa_ref, ob_ref, rb_ref, ga_ref, gb_ref, gob_ref, wo_ref, gf_ref,
                    wg_ref, wu_ref, wd_ref, y_ref, *, ff_chunks, row_parts):
    tm = x_ref.shape[0]
    parts = [slice(k * tm // row_parts, (k + 1) * tm // row_parts) for k in range(row_parts)]
    gob = gob_ref[...]

    def mix2(rows):
        ob = ob_ref[rows, :].astype(F32)
        obn = jnp.concatenate(
            [_rms_rows(ob[:, h * DV_B:(h + 1) * DV_B], gob) for h in range(N_HEADS_B)], axis=1)
        rbh = rb_ref[rows, :].astype(F32)
        obg = (obn * (rbh * (jnp.tanh(rbh) + 1.0))).astype(BF16)
        one = jnp.ones((), BF16)
        return ((jnp.tanh(ga_ref[rows, :]) + one) * oa_ref[rows, :]
                + (jnp.tanh(gb_ref[rows, :]) + one) * obg)

    mixes = [mix2(rows) for rows in parts]
    acc = [x_ref[rows, :] + jnp.dot(mx, wo_ref[...], preferred_element_type=F32)
           for rows, mx in zip(parts, mixes)]
    hs = [_rms_rows(x1, gf_ref[...]).astype(BF16) for x1 in acc]
    units = [(c0, c1, k) for c0, c1 in ff_chunks for k in range(row_parts)]
    act = {}
    for n in range(len(units) + DOWN_LAG):
        if n < len(units):
            c0, c1, k = units[n]
            gh = jnp.dot(hs[k], wg_ref[:, c0:c1], preferred_element_type=F32)
            u = jnp.dot(hs[k], wu_ref[:, c0:c1], preferred_element_type=F32)
            act[n] = ((gh * (jnp.tanh(gh) + 1.0)) * u).astype(BF16)
        if n >= DOWN_LAG:
            c0, c1, k = units[n - DOWN_LAG]
            acc[k] = acc[k] + jnp.dot(act.pop(n - DOWN_LAG), wd_ref[c0:c1, :], preferred_element_type=F32)
    for rows, y in zip(parts, acc):
        y_ref[rows, :] = y


def _ff_chunks(f, pref):
    mxu = 2 * LANES
    step = pref if f % mxu == 0 else f
    edges = list(range(0, f, step)) + [f]
    return tuple(zip(edges[:-1], edges[1:]))


def _merge_ffn(x, oa, ob, gates, gob, wo, gf, wg, wu, wd, l, *, tm):
    m, d = x.shape
    tile = lambda c: pl.BlockSpec((tm, d), lambda i: (i, c))
    return pl.pallas_call(
        functools.partial(_merge_ffn_body, ff_chunks=_ff_chunks(wg.shape[2], 1024),
                          row_parts=2 if tm % 32 == 0 else 1),
        grid=(m // tm,),
        in_specs=[tile(0), tile(0), tile(0), tile(0), tile(1), tile(2),
                  _resident(gob.shape), _layer(wo, l), _resident(gf.shape),
                  _layer(wg, l), _layer(wu, l), _layer(wd, l)],
        out_specs=tile(0),
        out_shape=jax.ShapeDtypeStruct((m, d), F32),
        compiler_params=_cparams("parallel"),
        name="merge_ffn",
    )(x, oa, ob, gates, gates, gates, gob, wo, gf, wg, wu, wd)


def _alibi_bias_t(cq, band, q_off, masked_prefixes):
    slopes = LOG2E * 2.0 ** (-8.0 * np.arange(1, N_HEADS_A + 1, dtype=np.float64) / N_HEADS_A)
    dist = np.abs(np.arange(band)[:, None] - q_off - np.arange(cq)[None, :])
    bias = -slopes.reshape(N_HEADS_A // 2, 1, 2, 1) * dist[None, :, None, :]
    bias = bias.reshape(N_HEADS_A // 2, band, 2 * cq)
    out = np.stack([np.where(np.arange(band)[None, :, None] < p, NEG, bias) for p in masked_prefixes])
    return jnp.asarray(out, F32)


def _sink_rows(sinks, cq):
    depth = sinks.shape[0]
    return jnp.repeat(sinks.reshape(depth, N_HEADS_A // 2, 2) * LOG2E, cq, axis=2)[:, :, None, :]


def _row_tile(m, pref):
    t = min(pref, m)
    while m % t:
        t //= 2
    return t


def kernel(x_prompt, x_sample, cache_swa_k, cache_swa_v, state_gla, w_in, w_a2, b_a, g_q, g_k, sinks,
           g_ob, w_out, g_mix, g_ffn, w_gate, w_up, w_down):
    depth = w_in.shape[0]
    bp, tp, d = x_prompt.shape
    bs, ts, _ = x_sample.shape
    past = cache_swa_k.shape[2]
    qw, kvw = N_HEADS_A * HEAD_DIM_A, N_KV_A * HEAD_DIM_A
    kw, vw = N_HEADS_B * DK_B, N_HEADS_B * DV_B
    assert past == WINDOW and tp % CHUNK == 0 and ts % GLA_BLOCK == 0 and ts <= CHUNK

    rb0 = qw + 2 * kvw + 2 * kw + vw
    a10 = rb0 + vw
    col_scale = np.ones((w_in.shape[2],), np.float32)
    col_scale[rb0:a10] = 0.5
    col_scale[a10 + GATE_RANK:] = 0.5
    w_main = _scale_cast(w_in, jnp.asarray(col_scale)[None], tr=_row_tile(d, 256))
    w_a2p = jnp.concatenate(
        [w_a2, jnp.zeros((depth, A1_PAD - GATE_RANK, kw), w_a2.dtype)], axis=1).astype(BF16)
    w_out_b, w_gate_b, w_up_b, w_down_b = (w.astype(BF16) for w in (0.5 * w_out, 0.5 * w_gate, w_up, w_down))
    gq2 = jnp.tile(g_q * (HEAD_DIM_A ** -0.5 * LOG2E), (1, LANES // HEAD_DIM_A))[:, None, :]
    gk2 = jnp.tile(g_k, (1, LANES // HEAD_DIM_A))[:, None, :]
    gob4 = g_ob[:, None, :]
    bias_p = _alibi_bias_t(CHUNK, WINDOW + CHUNK, WINDOW, (0, WINDOW, WINDOW - CHUNK))
    bias_s = _alibi_bias_t(ts, past + ts, past, (0,))
    sink_p = _sink_rows(sinks, CHUNK)
    sink_s = _sink_rows(sinks, ts)

    tq_p = _row_tile(tp, 1024)
    tm_p = _row_tile(bp * tp, 512)
    tm_s = _row_tile(bs * ts, 512)

    xp = x_prompt.reshape(bp * tp, d)
    xs = x_sample.reshape(bs * ts, d)
    pk, pv, sk, sv = [], [], [], []
    ps = ss = None
    cache_k = cache_swa_k.reshape(depth, bs, past, kvw)
    cache_v = cache_swa_v.reshape(depth, bs, past, kvw)
    for l in range(depth):
        dense = dict(gob=gob4[l], wo=w_out_b, gf=g_ffn[l][None], wg=w_gate_b, wu=w_up_b, wd=w_down_b, l=l)
        attn, gla_in, gates = _inproj(xp, g_mix[l][None], gq2[l], gk2[l], w_main, l, tm=tm_p)
        oa, kl, vl = _swa_prompt(attn, bias_p, sink_p[l], batch=bp, seq=tp, tq=tq_p)
        ob, ps = _gla(gla_in, w_a2p, b_a[l][None], None, ps, l, batch=bp, seq=tp, tq=tq_p, cs=CHUNK)
        xp = _merge_ffn(xp, oa, ob, gates, tm=tm_p, **dense)
        pk.append(kl.reshape(bp, WINDOW, N_KV_A, HEAD_DIM_A))
        pv.append(vl.reshape(bp, WINDOW, N_KV_A, HEAD_DIM_A))
        attn, gla_in, gates = _inproj(xs, g_mix[l][None], gq2[l], gk2[l], w_main, l, tm=tm_s)
        oa, kn, vn = _swa_sample(attn, cache_k, cache_v, l, bias_s, sink_s[l], batch=bs, tq=ts, nb=_row_tile(bs, 4))
        ob, ss = _gla(gla_in, w_a2p, b_a[l][None], state_gla, ss, l, batch=bs, seq=ts, tq=ts, cs=ts)
        xs = _merge_ffn(xs, oa, ob, gates, tm=tm_s, **dense)
        sk.append(kn.reshape(bs, ts, N_KV_A, HEAD_DIM_A))
        sv.append(vn.reshape(bs, ts, N_KV_A, HEAD_DIM_A))

    return (xp.reshape(bp, tp, d), xs.reshape(bs, ts, d), jnp.stack(pk), jnp.stack(pv), ps,
            jnp.stack(sk), jnp.stack(sv), ss)
```

```python
import functools

import jax
import jax.numpy as jnp
import numpy as np
from jax import lax
from jax.experimental import pallas as pl
from jax.experimental.pallas import tpu as pltpu

CHUNK = 64
EPS = 1e-6
NEG = -1e30
N_HEADS_A = 16
N_KV_A = 4
GROUP_A = N_HEADS_A // N_KV_A
HEAD_DIM_A = 64
WINDOW = 128
N_HEADS_B = 4
DK_B = 128
DV_B = 256
GATE_RANK = 16
GATE_TAU = 16.0
GLA_BLOCK = 16

LANES = 128
A1_PAD = LANES
VMEM_LIMIT = 56 * 1024 * 1024
SCORE_RING = 8
PROB_RING = 6
PAD_ROWS = 16
DOWN_LAG = 2
LOG2E = 1.4426950408889634

BF16 = jnp.bfloat16
F32 = jnp.float32
NT = (((1,), (1,)), ((), ()))
TN = (((0,), (0,)), ((), ()))


def _cparams(*sem):
    return pltpu.CompilerParams(dimension_semantics=sem, vmem_limit_bytes=VMEM_LIMIT)


def _resident(shape):
    nd = len(shape)
    return pl.BlockSpec(shape, lambda *_: (0,) * nd, pipeline_mode=pl.Buffered(1))


def _layer(stacked, l):
    nd = stacked.ndim - 1
    return pl.BlockSpec((None,) + stacked.shape[1:], lambda *_: (l,) + (0,) * nd, pipeline_mode=pl.Buffered(1))


def _rms_rows(x, g):
    ms = jnp.mean(x * x, axis=-1, keepdims=True)
    return (x * lax.rsqrt(ms + EPS)) * g


def _scale_cast_body(w_ref, o_ref, *, halved):
    tr = w_ref.shape[0]
    r = lax.broadcasted_iota(jnp.int32, (tr, 1), 0) + pl.program_id(1) * tr
    half = functools.reduce(jnp.logical_or, [(r >= lo) & (r < hi) for lo, hi in halved])
    o_ref[...] = (w_ref[...] * jnp.where(half, 0.5, 1.0)).astype(BF16)


def _scale_cast(w_t, halved, *, tr):
    depth, n, k = w_t.shape
    return pl.pallas_call(
        functools.partial(_scale_cast_body, halved=halved),
        grid=(depth, n // tr),
        in_specs=[pl.BlockSpec((None, tr, k), lambda l, i: (l, i, 0))],
        out_specs=pl.BlockSpec((None, tr, k), lambda l, i: (l, i, 0)),
        out_shape=jax.ShapeDtypeStruct((depth, n, k), BF16),
        compiler_params=_cparams("parallel", "parallel"),
        name="scale_cast",
    )(w_t)


def _head_norm_tile(x, g2, lo_half):
    sq = x * x
    ms_lo = jnp.sum(jnp.where(lo_half, sq, 0.0), axis=-1, keepdims=True) * (1.0 / HEAD_DIM_A)
    ms_hi = jnp.sum(jnp.where(lo_half, 0.0, sq), axis=-1, keepdims=True) * (1.0 / HEAD_DIM_A)
    r = jnp.where(lo_half, lax.rsqrt(ms_lo + EPS), lax.rsqrt(ms_hi + EPS))
    return (x * r) * g2


def _inproj_body(x_ref, g_ref, gq_ref, gk_ref, w_ref, oa_ref, ob_ref, oc_ref, *, plan):
    h = _rms_rows(x_ref[...], g_ref[...]).astype(BF16)
    lo_half = lax.broadcasted_iota(jnp.int32, (1, LANES), 1) < HEAD_DIM_A
    qw, kvw = N_HEADS_A * HEAD_DIM_A, N_KV_A * HEAD_DIM_A
    o_refs = (oa_ref, ob_ref, oc_ref)
    for c0, c1, dst, d0 in plan:
        acc = lax.dot_general(h, w_ref[c0:c1, :], NT, preferred_element_type=F32)
        if dst != 0:
            o_refs[dst][:, d0:d0 + (c1 - c0)] = acc.astype(BF16)
            continue
        for t0 in range(0, c1 - c0, LANES):
            col = d0 + t0
            tile = acc[:, t0:t0 + LANES]
            if col < qw:
                tile = _head_norm_tile(tile, gq_ref[...], lo_half)
            elif col < qw + kvw:
                tile = _head_norm_tile(tile, gk_ref[...], lo_half)
            oa_ref[:, col:col + LANES] = tile.astype(BF16)


def _inproj_plan(d, col_chunk):
    qw, kvw = N_HEADS_A * HEAD_DIM_A, N_KV_A * HEAD_DIM_A
    kw, vw = N_HEADS_B * DK_B, N_HEADS_B * DV_B
    attn_w, gla_w = qw + 2 * kvw, 2 * kw + vw
    a10 = attn_w + gla_w + vw
    runs = [(0, attn_w, 0, 0),
            (attn_w, attn_w + gla_w, 1, 0),
            (a10, a10 + A1_PAD, 1, gla_w),
            (attn_w + gla_w, a10, 2, 0),
            (a10 + GATE_RANK, a10 + GATE_RANK + 2 * d, 2, vw)]
    plan = []
    for c0, c1, dst, d0 in runs:
        for s0 in range(c0, c1, col_chunk):
            plan.append((s0, min(s0 + col_chunk, c1), dst, d0 + s0 - c0))
    return tuple(plan), (attn_w, gla_w + A1_PAD, vw + 2 * d)


def _inproj(x, g, gq2, gk2, w_main, l, *, tm):
    m, d = x.shape
    plan, widths = _inproj_plan(d, 512)
    return pl.pallas_call(
        functools.partial(_inproj_body, plan=plan),
        grid=(m // tm,),
        in_specs=[pl.BlockSpec((tm, d), lambda i: (i, 0)), _resident(g.shape),
                  _resident(gq2.shape), _resident(gk2.shape), _layer(w_main, l)],
        out_specs=[pl.BlockSpec((tm, w), lambda i: (i, 0)) for w in widths],
        out_shape=[jax.ShapeDtypeStruct((m, w), BF16) for w in widths],
        compiler_params=_cparams("parallel"),
        name="inproj",
    )(x, g, gq2, gk2, w_main)


def _dup_halves(x, lo_half):
    r = pltpu.roll(x, HEAD_DIM_A, axis=1)
    return jnp.where(lo_half, x, r), jnp.where(lo_half, r, x)


def _scores(lhs, kk, bias_t):
    return lax.dot_general(kk, lhs, NT, preferred_element_type=F32) + bias_t


def _sink_weights(s, sink_row):
    m = jnp.maximum(jnp.max(s, axis=0, keepdims=True), sink_row)
    p = jnp.exp2(s - m).astype(BF16)
    first = lax.broadcasted_iota(jnp.int32, (PAD_ROWS, s.shape[1]), 0) == 0
    tail = jnp.where(first, jnp.exp2(sink_row - m), 0.0).astype(BF16)
    return jnp.concatenate([p, tail], axis=0)


def _value_operand_tail(band):
    rows = band + PAD_ROWS
    counted = lax.broadcasted_iota(jnp.int32, (rows, LANES), 0) <= band
    return jnp.zeros((PAD_ROWS, LANES), BF16), jnp.where(counted, 1.0, 0.0).astype(BF16)


def _weighted_values(p, vv, tail, lo_half, cq):
    zero_rows, ones_blk = tail
    rhs = jnp.concatenate([jnp.concatenate([vv, zero_rows], axis=0), ones_blk], axis=1)
    o = lax.dot_general(p, rhs, TN, preferred_element_type=F32)
    num = jnp.where(lo_half, o[0:cq, 0:LANES], o[cq:2 * cq, 0:LANES])
    den = jnp.where(lo_half, o[0:cq, LANES:], o[cq:2 * cq, LANES:])
    return (num / den).astype(BF16)


def _half_masks():
    lane = lax.broadcasted_iota(jnp.int32, (1, LANES), 1)
    lo_half = lane < HEAD_DIM_A
    return lo_half, jnp.where(lo_half, 1.0, 0.0).astype(BF16), jnp.where(lo_half, 0.0, 1.0).astype(BF16)


def _swa_prompt_body(q_ref, kv_ref, bias_ref, sink_ref,
                     o_ref, klast_ref, vlast_ref, kk_s, vv_s, qm_s, s_buf, p_buf, *, tq):
    i = pl.program_id(1)
    kvw = N_KV_A * HEAD_DIM_A
    nchunk = tq // CHUNK
    lo_half, mask_lo, mask_hi = _half_masks()

    @pl.when(i == 0)
    def _():
        kk_s[:, 0:WINDOW, :] = jnp.zeros((N_KV_A, WINDOW, LANES), BF16)
        vv_s[:, 0:WINDOW, :] = jnp.zeros((N_KV_A, WINDOW, LANES), BF16)

    @pl.when(i > 0)
    def _():
        kk_s[:, 0:WINDOW, :] = kk_s[:, tq:tq + WINDOW, :]
        vv_s[:, 0:WINDOW, :] = vv_s[:, tq:tq + WINDOW, :]

    for t in range(kvw // LANES):
        k_t = kv_ref[:, t * LANES:(t + 1) * LANES].astype(F32)
        v_t = kv_ref[:, kvw + t * LANES:kvw + (t + 1) * LANES].astype(F32)
        klast_ref[0, :, t * LANES:(t + 1) * LANES] = k_t[tq - WINDOW:, :]
        vlast_ref[0, :, t * LANES:(t + 1) * LANES] = v_t[tq - WINDOW:, :]
        k0, k1 = _dup_halves(k_t, lo_half)
        v0, v1 = _dup_halves(v_t, lo_half)
        kk_s[2 * t, WINDOW:, :] = k0.astype(BF16)
        kk_s[2 * t + 1, WINDOW:, :] = k1.astype(BF16)
        vv_s[2 * t, WINDOW:, :] = v0.astype(BF16)
        vv_s[2 * t + 1, WINDOW:, :] = v1.astype(BF16)

    for t in range(N_HEADS_A * HEAD_DIM_A // LANES):
        q_t = q_ref[:, t * LANES:(t + 1) * LANES]
        qm_s[t, :, 0:CHUNK, :] = (q_t * mask_lo).reshape(nchunk, CHUNK, LANES)
        qm_s[t, :, CHUNK:, :] = (q_t * mask_hi).reshape(nchunk, CHUNK, LANES)

    band = WINDOW + CHUNK
    tail = _value_operand_tail(band)
    ntile = N_HEADS_A * HEAD_DIM_A // LANES
    nitem = nchunk * ntile
    n_sring, n_pring = s_buf.shape[0], p_buf.shape[0]
    skew_sm, skew_pv = n_sring - 2, n_pring - 1

    def keys(n):
        c, t = divmod(n, ntile)
        return t // 2, slice(c * CHUNK, c * CHUNK + band)

    def stage_scores(n):
        c, t = divmod(n, ntile)
        j, rows = keys(n)
        variant = jnp.where(i == 0, c + 1, 0) if c < WINDOW // CHUNK else 0
        s_buf[n % n_sring] = _scores(qm_s[t, c], kk_s[j, rows, :], bias_ref[variant, t])

    def stage_softmax(n):
        p_buf[n % n_pring] = _sink_weights(s_buf[n % n_sring], sink_ref[n % ntile])

    def stage_values(n):
        c, t = divmod(n, ntile)
        j, rows = keys(n)
        o_ref[c * CHUNK:(c + 1) * CHUNK, t * LANES:(t + 1) * LANES] = _weighted_values(
            p_buf[n % n_pring], vv_s[j, rows, :], tail, lo_half, CHUNK)

    for n in range(nitem + skew_sm + skew_pv):
        if n < nitem:
            stage_scores(n)
        if 0 <= n - skew_sm < nitem:
            stage_softmax(n - skew_sm)
        if 0 <= n - skew_sm - skew_pv < nitem:
            stage_values(n - skew_sm - skew_pv)


def _swa_prompt(attn, bias, sink_rows, *, batch, seq, tq):
    m = attn.shape[0]
    nt = seq // tq
    qw = N_HEADS_A * HEAD_DIM_A
    kvw = N_KV_A * HEAD_DIM_A
    row = lambda b, i: b * nt + i
    return pl.pallas_call(
        functools.partial(_swa_prompt_body, tq=tq),
        grid=(batch, nt),
        in_specs=[pl.BlockSpec((tq, qw), lambda b, i: (row(b, i), 0)),
                  pl.BlockSpec((tq, 2 * kvw), lambda b, i: (row(b, i), qw // (2 * kvw))),
                  _resident(bias.shape), _resident(sink_rows.shape)],
        out_specs=[pl.BlockSpec((tq, qw), lambda b, i: (row(b, i), 0)),
                   pl.BlockSpec((1, WINDOW, kvw), lambda b, i: (b, 0, 0)),
                   pl.BlockSpec((1, WINDOW, kvw), lambda b, i: (b, 0, 0))],
        out_shape=[jax.ShapeDtypeStruct((m, qw), BF16),
                   jax.ShapeDtypeStruct((batch, WINDOW, kvw), F32),
                   jax.ShapeDtypeStruct((batch, WINDOW, kvw), F32)],
        scratch_shapes=[pltpu.VMEM((N_KV_A, WINDOW + tq, LANES), BF16),
                        pltpu.VMEM((N_KV_A, WINDOW + tq, LANES), BF16),
                        pltpu.VMEM((qw // LANES, tq // CHUNK, 2 * CHUNK, LANES), BF16),
                        pltpu.VMEM((SCORE_RING, WINDOW + CHUNK, LANES), F32),
                        pltpu.VMEM((PROB_RING, WINDOW + CHUNK + PAD_ROWS, LANES), BF16)],
        compiler_params=_cparams("parallel", "arbitrary"),
        name="swa_prompt",
    )(attn, attn, bias, sink_rows)


def _swa_sample_body(q_ref, kv_ref, ck_ref, cv_ref, bias_ref, sink_ref,
                     o_ref, knew_ref, vnew_ref, *, tq, nb):
    kvw = N_KV_A * HEAD_DIM_A
    lo_half, mask_lo, mask_hi = _half_masks()
    tail = _value_operand_tail(ck_ref.shape[1] + tq)
    ntile = N_HEADS_A * HEAD_DIM_A // LANES
    vvs, scores = [], []
    for b in range(nb):
        rows = slice(b * tq, (b + 1) * tq)
        kk, vv = [None] * N_KV_A, [None] * N_KV_A
        for t in range(kvw // LANES):
            sl = slice(t * LANES, (t + 1) * LANES)
            k_t = kv_ref[rows, sl].astype(F32)
            v_t = kv_ref[rows, kvw + t * LANES:kvw + (t + 1) * LANES].astype(F32)
            knew_ref[b, :, sl] = k_t
            vnew_ref[b, :, sl] = v_t
            k0, k1 = _dup_halves(jnp.concatenate([ck_ref[b, :, sl], k_t], axis=0), lo_half)
            v0, v1 = _dup_halves(jnp.concatenate([cv_ref[b, :, sl], v_t], axis=0), lo_half)
            kk[2 * t], kk[2 * t + 1] = k0.astype(BF16), k1.astype(BF16)
            vv[2 * t], vv[2 * t + 1] = v0.astype(BF16), v1.astype(BF16)
        vvs.append(vv)
        for t in range(ntile):
            q_t = q_ref[rows, t * LANES:(t + 1) * LANES]
            lhs = jnp.concatenate([q_t * mask_lo, q_t * mask_hi], axis=0)
            scores.append(_scores(lhs, kk[t // 2], bias_ref[0, t]))
    weights = [_sink_weights(s, sink_ref[n % ntile]) for n, s in enumerate(scores)]
    for n, p in enumerate(weights):
        b, t = divmod(n, ntile)
        o_ref[b * tq:(b + 1) * tq, t * LANES:(t + 1) * LANES] = _weighted_values(
            p, vvs[b][t // 2], tail, lo_half, tq)


def _swa_sample(attn, cache_k, cache_v, l, bias, sink_rows, *, batch, tq, nb):
    m = attn.shape[0]
    qw = N_HEADS_A * HEAD_DIM_A
    kvw = N_KV_A * HEAD_DIM_A
    past = cache_k.shape[2]
    return pl.pallas_call(
        functools.partial(_swa_sample_body, tq=tq, nb=nb),
        grid=(batch // nb,),
        in_specs=[pl.BlockSpec((nb * tq, qw), lambda b: (b, 0)),
                  pl.BlockSpec((nb * tq, 2 * kvw), lambda b: (b, qw // (2 * kvw))),
                  pl.BlockSpec((None, nb, past, kvw), lambda b: (l, b, 0, 0)),
                  pl.BlockSpec((None, nb, past, kvw), lambda b: (l, b, 0, 0)),
                  _resident(bias.shape), _resident(sink_rows.shape)],
        out_specs=[pl.BlockSpec((nb * tq, qw), lambda b: (b, 0)),
                   pl.BlockSpec((nb, tq, kvw), lambda b: (b, 0, 0)),
                   pl.BlockSpec((nb, tq, kvw), lambda b: (b, 0, 0))],
        out_shape=[jax.ShapeDtypeStruct((m, qw), BF16),
                   jax.ShapeDtypeStruct((batch, tq, kvw), F32),
                   jax.ShapeDtypeStruct((batch, tq, kvw), F32)],
        compiler_params=_cparams("parallel"),
        name="swa_sample",
    )(attn, attn, cache_k, cache_v, bias, sink_rows)


def _log_sigmoid(z):
    return -(jnp.maximum(-z, 0.0) + jnp.log(1.0 + jnp.exp(-jnp.abs(z))))


def _gla_body(*refs, tq, cs, has_init, has_alias):
    refs = list(refs)
    qk_ref, v_ref, a1_ref, wa2_ref, ba_ref = refs[:5]
    s0_ref = refs[5] if has_init else None
    (o_ref, sout_ref, st_s, cum_s, qin_s, kout_s, qs_s, ks_s, dec_s, att_s, oi_s,
     upd_s) = refs[5 + has_init + has_alias:]
    i = pl.program_id(1)
    kw = N_HEADS_B * DK_B
    nchunk = tq // cs
    nsb = cs // GLA_BLOCK
    scale = DK_B ** -0.5
    chunk_rows = [slice(c * cs, (c + 1) * cs) for c in range(nchunk)]
    ksls = [slice(h * DK_B, (h + 1) * DK_B) for h in range(N_HEADS_B)]
    vsls = [slice(h * DV_B, (h + 1) * DV_B) for h in range(N_HEADS_B)]

    @pl.when(i == 0)
    def _():
        for h in range(N_HEADS_B):
            if has_init:
                st_s[h] = s0_ref[0, h].T
            else:
                st_s[h] = jnp.zeros((DV_B, DK_B), F32)

    z = jnp.dot(a1_ref[...], wa2_ref[...], preferred_element_type=F32) + ba_ref[...]
    la = _log_sigmoid(z) * (1.0 / GATE_TAU)
    row = lax.broadcasted_iota(jnp.int32, (cs, cs), 0)
    col = lax.broadcasted_iota(jnp.int32, (cs, cs), 1)
    causal = col <= row
    tri = jnp.where(causal, 1.0, 0.0).astype(BF16)
    la_hi = la.astype(BF16)
    la_lo = (la - la_hi.astype(F32)).astype(BF16)

    def stage1(c):
        rows = chunk_rows[c]
        cum_s[rows, :] = (jnp.dot(tri, la_hi[rows], preferred_element_type=F32)
                          + jnp.dot(tri, la_lo[rows], preferred_element_type=F32))

    def stage2(c):
        rows = chunk_rows[c]
        cum = cum_s[rows, :]
        q = qk_ref[rows, 0:kw]
        k = qk_ref[rows, kw:2 * kw]
        last = cum[cs - 1:cs, :]
        qin_s[rows, :] = q * (jnp.exp(cum) * scale).astype(BF16)
        kout_s[rows, :] = k * jnp.exp(last - cum).astype(BF16)
        dec_s[c] = jnp.exp(last)
        for sb in range(nsb):
            lo, hi = sb * GLA_BLOCK, (sb + 1) * GLA_BLOCK
            ref_row = cum[lo - 1:lo, :] if sb > 0 else jnp.zeros_like(last)
            qs_s[c * cs + lo:c * cs + hi, :] = q[lo:hi] * (jnp.exp(cum[lo:hi] - ref_row) * scale).astype(BF16)
            ks_s[sb, c * cs:c * cs + hi, :] = k[0:hi] * jnp.exp(ref_row - cum[0:hi]).astype(BF16)
            if hi < cs:
                ks_s[sb, c * cs + hi:(c + 1) * cs, :] = jnp.zeros((cs - hi, kw), BF16)

    def stage3a(c):
        rows = chunk_rows[c]
        for h in range(N_HEADS_B):
            att = jnp.concatenate(
                [lax.dot_general(qs_s[c * cs + sb * GLA_BLOCK:c * cs + (sb + 1) * GLA_BLOCK, ksls[h]],
                                 ks_s[sb, rows, ksls[h]], NT, preferred_element_type=F32) for sb in range(nsb)],
                axis=0)
            att_s[c, h] = jnp.where(causal, att, 0.0).astype(BF16)

    def stage3b(c):
        rows = chunk_rows[c]
        for h in range(N_HEADS_B):
            oi_s[rows, vsls[h]] = jnp.dot(att_s[c, h], v_ref[rows, vsls[h]], preferred_element_type=F32)
            upd_s[c, h] = lax.dot_general(v_ref[rows, vsls[h]], kout_s[rows, ksls[h]], TN,
                                          preferred_element_type=F32)

    def stage4(c):
        rows = chunk_rows[c]
        dec = dec_s[c]
        for h in range(N_HEADS_B):
            st = st_s[h]
            o = oi_s[rows, vsls[h]] + lax.dot_general(qin_s[rows, ksls[h]], st.astype(BF16), NT,
                                                      preferred_element_type=F32)
            o_ref[rows, vsls[h]] = o.astype(BF16)
            st_s[h] = st * dec[:, ksls[h]] + upd_s[c, h]

    stages = (stage1, stage2, stage3a, stage3b, stage4)
    for n in range(nchunk + len(stages) - 1):
        for depth, stage in enumerate(stages):
            if 0 <= n - depth < nchunk:
                stage(n - depth)

    @pl.when(i == pl.num_programs(1) - 1)
    def _():
        for h in range(N_HEADS_B):
            sout_ref[0, h] = st_s[h].T


def _gla(gla_in, wa2p, ba, s0, states, l, *, batch, seq, tq, cs):
    depth = wa2p.shape[0]
    m = gla_in.shape[0]
    nt = seq // tq
    kw = N_HEADS_B * DK_B
    vw = N_HEADS_B * DV_B
    row = lambda b, i: b * nt + i
    has_init = s0 is not None
    in_specs = [pl.BlockSpec((tq, 2 * kw), lambda b, i: (row(b, i), 0)),
                pl.BlockSpec((tq, vw), lambda b, i: (row(b, i), 2 * kw // vw)),
                pl.BlockSpec((tq, A1_PAD), lambda b, i: (row(b, i), (2 * kw + vw) // A1_PAD)),
                _layer(wa2p, l), _resident(ba.shape)]
    args = [gla_in, gla_in, gla_in, wa2p, ba]
    state_spec = pl.BlockSpec((None, 1, N_HEADS_B, DK_B, DV_B), lambda b, i: (l, b, 0, 0, 0))
    if has_init:
        in_specs.append(state_spec)
        args.append(s0)
    aliases = {}
    if states is not None:
        aliases[len(args)] = 1
        in_specs.append(pl.BlockSpec(memory_space=pl.ANY))
        args.append(states)
    nchunk = tq // cs
    return pl.pallas_call(
        functools.partial(_gla_body, tq=tq, cs=cs, has_init=has_init, has_alias=states is not None),
        grid=(batch, nt),
        in_specs=in_specs,
        out_specs=[pl.BlockSpec((tq, vw), lambda b, i: (row(b, i), 0)), state_spec],
        out_shape=[jax.ShapeDtypeStruct((m, vw), BF16),
                   jax.ShapeDtypeStruct((depth, batch, N_HEADS_B, DK_B, DV_B), F32)],
        input_output_aliases=aliases,
        scratch_shapes=[pltpu.VMEM((N_HEADS_B, DV_B, DK_B), F32),
                        pltpu.VMEM((tq, kw), F32),
                        pltpu.VMEM((tq, kw), BF16),
                        pltpu.VMEM((tq, kw), BF16),
                        pltpu.VMEM((tq, kw), BF16),
                        pltpu.VMEM((cs // GLA_BLOCK, tq, kw), BF16),
                        pltpu.VMEM((nchunk, 1, kw), F32),
                        pltpu.VMEM((nchunk, N_HEADS_B, cs, cs), BF16),
                        pltpu.VMEM((tq, vw), F32),
                        pltpu.VMEM((nchunk, N_HEADS_B, DV_B, DK_B), F32)],
        compiler_params=_cparams("parallel", "arbitrary"),
        name="gla_init" if has_init else "gla",
    )(*args)


def _merge_ffn_body(x_ref, oa_ref, ob_ref, rb_ref, ga_ref, gb_ref, gob_ref, wo_ref, gf_ref,
                    wg_ref, wu_ref, wd_ref, y_ref, *, ff_chunks, row_parts):
    tm = x_ref.shape[0]
    parts = [slice(k * tm // row_parts, (k + 1) * tm // row_parts) for k in range(row_parts)]
    gob = gob_ref[...]

    def mix2(rows):
        ob = ob_ref[rows, :].astype(F32)
        obn = jnp.concatenate(
            [_rms_rows(ob[:, h * DV_B:(h + 1) * DV_B], gob) for h in range(N_HEADS_B)], axis=1)
        rbh = rb_ref[rows, :].astype(F32)
        obg = (obn * (rbh * (jnp.tanh(rbh) + 1.0))).astype(BF16)
        one = jnp.ones((), BF16)
        return ((jnp.tanh(ga_ref[rows, :]) + one) * oa_ref[rows, :]
                + (jnp.tanh(gb_ref[rows, :]) + one) * obg)

    mixes = [mix2(rows) for rows in parts]
    acc = [x_ref[rows, :] + jnp.dot(mx, wo_ref[...], preferred_element_type=F32)
           for rows, mx in zip(parts, mixes)]
    hs = [_rms_rows(x1, gf_ref[...]).astype(BF16) for x1 in acc]
    units = [(c0, c1, k) for c0, c1 in ff_chunks for k in range(row_parts)]
    act = {}
    for n in range(len(units) + DOWN_LAG):
        if n < len(units):
            c0, c1, k = units[n]
            gh = jnp.dot(hs[k], wg_ref[:, c0:c1], preferred_element_type=F32)
            u = jnp.dot(hs[k], wu_ref[:, c0:c1], preferred_element_type=F32)
            act[n] = ((gh * (jnp.tanh(gh) + 1.0)) * u).astype(BF16)
        if n >= DOWN_LAG:
            c0, c1, k = units[n - DOWN_LAG]
            acc[k] = acc[k] + jnp.dot(act.pop(n - DOWN_LAG), wd_ref[c0:c1, :], preferred_element_type=F32)
    for rows, y in zip(parts, acc):
        y_ref[rows, :] = y


def _ff_chunks(f, pref):
    mxu = 2 * LANES
    step = pref if f % mxu == 0 else f
    edges = list(range(0, f, step)) + [f]
    return tuple(zip(edges[:-1], edges[1:]))


def _merge_ffn(x, oa, ob, gates, gob, wo, gf, wg, wu, wd, l, *, tm):
    m, d = x.shape
    tile = lambda c: pl.BlockSpec((tm, d), lambda i: (i, c))
    return pl.pallas_call(
        functools.partial(_merge_ffn_body, ff_chunks=_ff_chunks(wg.shape[2], 1024),
                          row_parts=2 if tm % 32 == 0 else 1),
        grid=(m // tm,),
        in_specs=[tile(0), tile(0), tile(0), tile(0), tile(1), tile(2),
                  _resident(gob.shape), _layer(wo, l), _resident(gf.shape),
                  _layer(wg, l), _layer(wu, l), _layer(wd, l)],
        out_specs=tile(0),
        out_shape=jax.ShapeDtypeStruct((m, d), F32),
        compiler_params=_cparams("parallel"),
        name="merge_ffn",
    )(x, oa, ob, gates, gates, gates, gob, wo, gf, wg, wu, wd)


def _alibi_bias_t(cq, band, q_off, masked_prefixes):
    slopes = LOG2E * 2.0 ** (-8.0 * np.arange(1, N_HEADS_A + 1, dtype=np.float64) / N_HEADS_A)
    dist = np.abs(np.arange(band)[:, None] - q_off - np.arange(cq)[None, :])
    bias = -slopes.reshape(N_HEADS_A // 2, 1, 2, 1) * dist[None, :, None, :]
    bias = bias.reshape(N_HEADS_A // 2, band, 2 * cq)
    out = np.stack([np.where(np.arange(band)[None, :, None] < p, NEG, bias) for p in masked_prefixes])
    return jnp.asarray(out, F32)


def _sink_rows(sinks, cq):
    depth = sinks.shape[0]
    return jnp.repeat(sinks.reshape(depth, N_HEADS_A // 2, 2) * LOG2E, cq, axis=2)[:, :, None, :]


def _row_tile(m, pref):
    t = min(pref, m)
    while m % t:
        t //= 2
    return t


def kernel(x_prompt, x_sample, cache_swa_k, cache_swa_v, state_gla, w_in, w_a2, b_a, g_q, g_k, sinks,
           g_ob, w_out, g_mix, g_ffn, w_gate, w_up, w_down):
    depth = w_in.shape[0]
    bp, tp, d = x_prompt.shape
    bs, ts, _ = x_sample.shape
    past = cache_swa_k.shape[2]
    qw, kvw = N_HEADS_A * HEAD_DIM_A, N_KV_A * HEAD_DIM_A
    kw, vw = N_HEADS_B * DK_B, N_HEADS_B * DV_B
    assert past == WINDOW and tp % CHUNK == 0 and ts % GLA_BLOCK == 0 and ts <= CHUNK

    rb0 = qw + 2 * kvw + 2 * kw + vw
    a10 = rb0 + vw
    p_in = w_in.shape[2]
    tr = next(t for t in range(p_in // 2, 0, -1) if p_in % t == 0 and t % 16 == 0)
    w_main = _scale_cast(jnp.swapaxes(w_in, 1, 2), ((rb0, a10), (a10 + GATE_RANK, p_in)), tr=tr)
    w_a2p = jnp.concatenate(
        [w_a2, jnp.zeros((depth, A1_PAD - GATE_RANK, kw), w_a2.dtype)], axis=1).astype(BF16)
    w_out_b, w_gate_b, w_up_b, w_down_b = (w.astype(BF16) for w in (0.5 * w_out, 0.5 * w_gate, w_up, w_down))
    gq2 = jnp.tile(g_q * (HEAD_DIM_A ** -0.5 * LOG2E), (1, LANES // HEAD_DIM_A))[:, None, :]
    gk2 = jnp.tile(g_k, (1, LANES // HEAD_DIM_A))[:, None, :]
    gob4 = g_ob[:, None, :]
    bias_p = _alibi_bias_t(CHUNK, WINDOW + CHUNK, WINDOW, (0, WINDOW, WINDOW - CHUNK))
    bias_s = _alibi_bias_t(ts, past + ts, past, (0,))
    sink_p = _sink_rows(sinks, CHUNK)
    sink_s = _sink_rows(sinks, ts)

    tq_p = _row_tile(tp, 1024)
    tm_p = _row_tile(bp * tp, 512)
    tm_s = _row_tile(bs * ts, 512)

    xp = x_prompt.reshape(bp * tp, d)
    xs = x_sample.reshape(bs * ts, d)
    pk, pv, sk, sv = [], [], [], []
    ps = ss = None
    cache_k = cache_swa_k.reshape(depth, bs, past, kvw)
    cache_v = cache_swa_v.reshape(depth, bs, past, kvw)
    for l in range(depth):
        dense = dict(gob=gob4[l], wo=w_out_b, gf=g_ffn[l][None], wg=w_gate_b, wu=w_up_b, wd=w_down_b, l=l)
        attn, gla_in, gates = _inproj(xp, g_mix[l][None], gq2[l], gk2[l], w_main, l, tm=tm_p)
        oa, kl, vl = _swa_prompt(attn, bias_p, sink_p[l], batch=bp, seq=tp, tq=tq_p)
        ob, ps = _gla(gla_in, w_a2p, b_a[l][None], None, ps, l, batch=bp, seq=tp, tq=tq_p, cs=CHUNK)
        xp = _merge_ffn(xp, oa, ob, gates, tm=tm_p, **dense)
        pk.append(kl.reshape(bp, WINDOW, N_KV_A, HEAD_DIM_A))
        pv.append(vl.reshape(bp, WINDOW, N_KV_A, HEAD_DIM_A))
        attn, gla_in, gates = _inproj(xs, g_mix[l][None], gq2[l], gk2[l], w_main, l, tm=tm_s)
        oa, kn, vn = _swa_sample(attn, cache_k, cache_v, l, bias_s, sink_s[l], batch=bs, tq=ts, nb=_row_tile(bs, 4))
        ob, ss = _gla(gla_in, w_a2p, b_a[l][None], state_gla, ss, l, batch=bs, seq=ts, tq=ts, cs=ts)
        xs = _merge_ffn(xs, oa, ob, gates, tm=tm_s, **dense)
        sk.append(kn.reshape(bs, ts, N_KV_A, HEAD_DIM_A))
        sv.append(vn.reshape(bs, ts, N_KV_A, HEAD_DIM_A))

    return (xp.reshape(bp, tp, d), xs.reshape(bs, ts, d), jnp.stack(pk), jnp.stack(pv), ps,
            jnp.stack(sk), jnp.stack(sv), ss)
```

```python
import functools

import jax
import jax.numpy as jnp
import numpy as np
from jax import lax
from jax.experimental import pallas as pl
from jax.experimental.pallas import tpu as pltpu

CHUNK = 64
EPS = 1e-6
NEG = -1e30
N_HEADS_A = 16
N_KV_A = 4
GROUP_A = N_HEADS_A // N_KV_A
HEAD_DIM_A = 64
WINDOW = 128
N_HEADS_B = 4
DK_B = 128
DV_B = 256
GATE_RANK = 16
GATE_TAU = 16.0
GLA_BLOCK = 16

LANES = 128
A1_PAD = LANES
VMEM_LIMIT = 56 * 1024 * 1024
SCORE_RING = 8
PROB_RING = 6
PAD_ROWS = 16
DOWN_LAG = 2
LOG2E = 1.4426950408889634

BF16 = jnp.bfloat16
F32 = jnp.float32
NT = (((1,), (1,)), ((), ()))
TN = (((0,), (0,)), ((), ()))


def _cparams(*sem):
    return pltpu.CompilerParams(dimension_semantics=sem, vmem_limit_bytes=VMEM_LIMIT)


def _resident(shape):
    nd = len(shape)
    return pl.BlockSpec(shape, lambda *_: (0,) * nd, pipeline_mode=pl.Buffered(1))


def _layer(stacked, l):
    nd = stacked.ndim - 1
    return pl.BlockSpec((None,) + stacked.shape[1:], lambda *_: (l,) + (0,) * nd, pipeline_mode=pl.Buffered(1))


def _rms_rows(x, g):
    ms = jnp.mean(x * x, axis=-1, keepdims=True)
    return (x * lax.rsqrt(ms + EPS)) * g


def _scale_cast_body(w_ref, o_ref, *, halved):
    tr = w_ref.shape[0]
    r = lax.broadcasted_iota(jnp.int32, (tr, 1), 0) + pl.program_id(1) * tr
    half = functools.reduce(jnp.logical_or, [(r >= lo) & (r < hi) for lo, hi in halved])
    o_ref[...] = (w_ref[...] * jnp.where(half, 0.5, 1.0)).astype(BF16)


def _scale_cast(w_t, halved, *, tr):
    depth, n, k = w_t.shape
    return pl.pallas_call(
        functools.partial(_scale_cast_body, halved=halved),
        grid=(depth, n // tr),
        in_specs=[pl.BlockSpec((None, tr, k), lambda l, i: (l, i, 0))],
        out_specs=pl.BlockSpec((None, tr, k), lambda l, i: (l, i, 0)),
        out_shape=jax.ShapeDtypeStruct((depth, n, k), BF16),
        compiler_params=_cparams("parallel", "parallel"),
        name="scale_cast",
    )(w_t)


def _head_norm_tile(x, g2, lo_half):
    sq = x * x
    ms_lo = jnp.sum(jnp.where(lo_half, sq, 0.0), axis=-1, keepdims=True) * (1.0 / HEAD_DIM_A)
    ms_hi = jnp.sum(jnp.where(lo_half, 0.0, sq), axis=-1, keepdims=True) * (1.0 / HEAD_DIM_A)
    r = jnp.where(lo_half, lax.rsqrt(ms_lo + EPS), lax.rsqrt(ms_hi + EPS))
    return (x * r) * g2


def _inproj_body(x_ref, g_ref, gq_ref, gk_ref, w_ref, oa_ref, ob_ref, oc_ref, *, plan):
    h = _rms_rows(x_ref[...], g_ref[...]).astype(BF16)
    lo_half = lax.broadcasted_iota(jnp.int32, (1, LANES), 1) < HEAD_DIM_A
    qw, kvw = N_HEADS_A * HEAD_DIM_A, N_KV_A * HEAD_DIM_A
    o_refs = (oa_ref, ob_ref, oc_ref)
    for c0, c1, dst, d0 in plan:
        acc = lax.dot_general(h, w_ref[c0:c1, :], NT, preferred_element_type=F32)
        if dst != 0:
            o_refs[dst][:, d0:d0 + (c1 - c0)] = acc.astype(BF16)
            continue
        for t0 in range(0, c1 - c0, LANES):
            col = d0 + t0
            tile = acc[:, t0:t0 + LANES]
            if col < qw:
                tile = _head_norm_tile(tile, gq_ref[...], lo_half)
            elif col < qw + kvw:
                tile = _head_norm_tile(tile, gk_ref[...], lo_half)
            oa_ref[:, col:col + LANES] = tile.astype(BF16)


def _inproj_plan(d, col_chunk):
    qw, kvw = N_HEADS_A * HEAD_DIM_A, N_KV_A * HEAD_DIM_A
    kw, vw = N_HEADS_B * DK_B, N_HEADS_B * DV_B
    attn_w, gla_w = qw + 2 * kvw, 2 * kw + vw
    a10 = attn_w + gla_w + vw
    runs = [(0, attn_w, 0, 0),
            (attn_w, attn_w + gla_w, 1, 0),
            (a10, a10 + A1_PAD, 1, gla_w),
            (attn_w + gla_w, a10, 2, 0),
            (a10 + GATE_RANK, a10 + GATE_RANK + 2 * d, 2, vw)]
    plan = []
    for c0, c1, dst, d0 in runs:
        for s0 in range(c0, c1, col_chunk):
            plan.append((s0, min(s0 + col_chunk, c1), dst, d0 + s0 - c0))
    return tuple(plan), (attn_w, gla_w + A1_PAD, vw + 2 * d)


def _inproj(x, g, gq2, gk2, w_main, l, *, tm):
    m, d = x.shape
    plan, widths = _inproj_plan(d, 512)
    return pl.pallas_call(
        functools.partial(_inproj_body, plan=plan),
        grid=(m // tm,),
        in_specs=[pl.BlockSpec((tm, d), lambda i: (i, 0)), _resident(g.shape),
                  _resident(gq2.shape), _resident(gk2.shape), _layer(w_main, l)],
        out_specs=[pl.BlockSpec((tm, w), lambda i: (i, 0)) for w in widths],
        out_shape=[jax.ShapeDtypeStruct((m, w), BF16) for w in widths],
        compiler_params=_cparams("parallel"),
        name="inproj",
    )(x, g, gq2, gk2, w_main)


def _dup_halves(x, lo_half):
    r = pltpu.roll(x, HEAD_DIM_A, axis=1)
    return jnp.where(lo_half, x, r), jnp.where(lo_half, r, x)


def _scores(lhs, kk, bias_t):
    return lax.dot_general(kk, lhs, NT, preferred_element_type=F32) + bias_t


def _sink_weights(s, sink_row):
    m = jnp.maximum(jnp.max(s, axis=0, keepdims=True), sink_row)
    p = jnp.exp2(s - m).astype(BF16)
    first = lax.broadcasted_iota(jnp.int32, (PAD_ROWS, s.shape[1]), 0) == 0
    tail = jnp.where(first, jnp.exp2(sink_row - m), 0.0).astype(BF16)
    return jnp.concatenate([p, tail], axis=0)


def _value_operand_tail(band):
    rows = band + PAD_ROWS
    counted = lax.broadcasted_iota(jnp.int32, (rows, LANES), 0) <= band
    return jnp.zeros((PAD_ROWS, LANES), BF16), jnp.where(counted, 1.0, 0.0).astype(BF16)


def _weighted_values(p, vv, tail, lo_half, cq):
    zero_rows, ones_blk = tail
    rhs = jnp.concatenate([jnp.concatenate([vv, zero_rows], axis=0), ones_blk], axis=1)
    o = lax.dot_general(p, rhs, TN, preferred_element_type=F32)
    num = jnp.where(lo_half, o[0:cq, 0:LANES], o[cq:2 * cq, 0:LANES])
    den = jnp.where(lo_half, o[0:cq, LANES:], o[cq:2 * cq, LANES:])
    return (num / den).astype(BF16)


def _half_masks():
    lane = lax.broadcasted_iota(jnp.int32, (1, LANES), 1)
    lo_half = lane < HEAD_DIM_A
    return lo_half, jnp.where(lo_half, 1.0, 0.0).astype(BF16), jnp.where(lo_half, 0.0, 1.0).astype(BF16)


def _swa_prompt_body(q_ref, kv_ref, bias_ref, sink_ref,
                     o_ref, klast_ref, vlast_ref, kk_s, vv_s, qm_s, s_buf, p_buf, *, tq):
    i = pl.program_id(1)
    kvw = N_KV_A * HEAD_DIM_A
    nchunk = tq // CHUNK
    lo_half, mask_lo, mask_hi = _half_masks()

    @pl.when(i == 0)
    def _():
        kk_s[:, 0:WINDOW, :] = jnp.zeros((N_KV_A, WINDOW, LANES), BF16)
        vv_s[:, 0:WINDOW, :] = jnp.zeros((N_KV_A, WINDOW, LANES), BF16)

    @pl.when(i > 0)
    def _():
        kk_s[:, 0:WINDOW, :] = kk_s[:, tq:tq + WINDOW, :]
        vv_s[:, 0:WINDOW, :] = vv_s[:, tq:tq + WINDOW, :]

    for t in range(kvw // LANES):
        k_t = kv_ref[:, t * LANES:(t + 1) * LANES].astype(F32)
        v_t = kv_ref[:, kvw + t * LANES:kvw + (t + 1) * LANES].astype(F32)
        klast_ref[0, :, t * LANES:(t + 1) * LANES] = k_t[tq - WINDOW:, :]
        vlast_ref[0, :, t * LANES:(t + 1) * LANES] = v_t[tq - WINDOW:, :]
        k0, k1 = _dup_halves(k_t, lo_half)
        v0, v1 = _dup_halves(v_t, lo_half)
        kk_s[2 * t, WINDOW:, :] = k0.astype(BF16)
        kk_s[2 * t + 1, WINDOW:, :] = k1.astype(BF16)
        vv_s[2 * t, WINDOW:, :] = v0.astype(BF16)
        vv_s[2 * t + 1, WINDOW:, :] = v1.astype(BF16)

    for t in range(N_HEADS_A * HEAD_DIM_A // LANES):
        q_t = q_ref[:, t * LANES:(t + 1) * LANES]
        qm_s[t, :, 0:CHUNK, :] = (q_t * mask_lo).reshape(nchunk, CHUNK, LANES)
        qm_s[t, :, CHUNK:, :] = (q_t * mask_hi).reshape(nchunk, CHUNK, LANES)

    band = WINDOW + CHUNK
    tail = _value_operand_tail(band)
    ntile = N_HEADS_A * HEAD_DIM_A // LANES
    nitem = nchunk * ntile
    n_sring, n_pring = s_buf.shape[0], p_buf.shape[0]
    skew_sm, skew_pv = n_sring - 2, n_pring - 1

    def keys(n):
        c, t = divmod(n, ntile)
        return t // 2, slice(c * CHUNK, c * CHUNK + band)

    def stage_scores(n):
        c, t = divmod(n, ntile)
        j, rows = keys(n)
        variant = jnp.where(i == 0, c + 1, 0) if c < WINDOW // CHUNK else 0
        s_buf[n % n_sring] = _scores(qm_s[t, c], kk_s[j, rows, :], bias_ref[variant, t])

    def stage_softmax(n):
        p_buf[n % n_pring] = _sink_weights(s_buf[n % n_sring], sink_ref[n % ntile])

    def stage_values(n):
        c, t = divmod(n, ntile)
        j, rows = keys(n)
        o_ref[c * CHUNK:(c + 1) * CHUNK, t * LANES:(t + 1) * LANES] = _weighted_values(
            p_buf[n % n_pring], vv_s[j, rows, :], tail, lo_half, CHUNK)

    for n in range(nitem + skew_sm + skew_pv):
        if n < nitem:
            stage_scores(n)
        if 0 <= n - skew_sm < nitem:
            stage_softmax(n - skew_sm)
        if 0 <= n - skew_sm - skew_pv < nitem:
            stage_values(n - skew_sm - skew_pv)


def _swa_prompt(attn, bias, sink_rows, *, batch, seq, tq):
    m = attn.shape[0]
    nt = seq // tq
    qw = N_HEADS_A * HEAD_DIM_A
    kvw = N_KV_A * HEAD_DIM_A
    row = lambda b, i: b * nt + i
    return pl.pallas_call(
        functools.partial(_swa_prompt_body, tq=tq),
        grid=(batch, nt),
        in_specs=[pl.BlockSpec((tq, qw), lambda b, i: (row(b, i), 0)),
                  pl.BlockSpec((tq, 2 * kvw), lambda b, i: (row(b, i), qw // (2 * kvw))),
                  _resident(bias.shape), _resident(sink_rows.shape)],
        out_specs=[pl.BlockSpec((tq, qw), lambda b, i: (row(b, i), 0)),
                   pl.BlockSpec((1, WINDOW, kvw), lambda b, i: (b, 0, 0)),
                   pl.BlockSpec((1, WINDOW, kvw), lambda b, i: (b, 0, 0))],
        out_shape=[jax.ShapeDtypeStruct((m, qw), BF16),
                   jax.ShapeDtypeStruct((batch, WINDOW, kvw), F32),
                   jax.ShapeDtypeStruct((batch, WINDOW, kvw), F32)],
        scratch_shapes=[pltpu.VMEM((N_KV_A, WINDOW + tq, LANES), BF16),
                        pltpu.VMEM((N_KV_A, WINDOW + tq, LANES), BF16),
                        pltpu.VMEM((qw // LANES, tq // CHUNK, 2 * CHUNK, LANES), BF16),
                        pltpu.VMEM((SCORE_RING, WINDOW + CHUNK, LANES), F32),
                        pltpu.VMEM((PROB_RING, WINDOW + CHUNK + PAD_ROWS, LANES), BF16)],
        compiler_params=_cparams("parallel", "arbitrary"),
        name="swa_prompt",
    )(attn, attn, bias, sink_rows)


def _swa_sample_body(q_ref, kv_ref, ck_ref, cv_ref, bias_ref, sink_ref,
                     o_ref, knew_ref, vnew_ref, *, tq, nb):
    kvw = N_KV_A * HEAD_DIM_A
    lo_half, mask_lo, mask_hi = _half_masks()
    tail = _value_operand_tail(ck_ref.shape[1] + tq)
    ntile = N_HEADS_A * HEAD_DIM_A // LANES
    vvs, scores = [], []
    for b in range(nb):
        rows = slice(b * tq, (b + 1) * tq)
        kk, vv = [None] * N_KV_A, [None] * N_KV_A
        for t in range(kvw // LANES):
            sl = slice(t * LANES, (t + 1) * LANES)
            k_t = kv_ref[rows, sl].astype(F32)
            v_t = kv_ref[rows, kvw + t * LANES:kvw + (t + 1) * LANES].astype(F32)
            knew_ref[b, :, sl] = k_t
            vnew_ref[b, :, sl] = v_t
            k0, k1 = _dup_halves(jnp.concatenate([ck_ref[b, :, sl], k_t], axis=0), lo_half)
            v0, v1 = _dup_halves(jnp.concatenate([cv_ref[b, :, sl], v_t], axis=0), lo_half)
            kk[2 * t], kk[2 * t + 1] = k0.astype(BF16), k1.astype(BF16)
            vv[2 * t], vv[2 * t + 1] = v0.astype(BF16), v1.astype(BF16)
        vvs.append(vv)
        for t in range(ntile):
            q_t = q_ref[rows, t * LANES:(t + 1) * LANES]
            lhs = jnp.concatenate([q_t * mask_lo, q_t * mask_hi], axis=0)
            scores.append(_scores(lhs, kk[t // 2], bias_ref[0, t]))
    weights = [_sink_weights(s, sink_ref[n % ntile]) for n, s in enumerate(scores)]
    for n, p in enumerate(weights):
        b, t = divmod(n, ntile)
        o_ref[b * tq:(b + 1) * tq, t * LANES:(t + 1) * LANES] = _weighted_values(
            p, vvs[b][t // 2], tail, lo_half, tq)


def _swa_sample(attn, cache_k, cache_v, l, bias, sink_rows, *, batch, tq, nb):
    m = attn.shape[0]
    qw = N_HEADS_A * HEAD_DIM_A
    kvw = N_KV_A * HEAD_DIM_A
    past = cache_k.shape[2]
    return pl.pallas_call(
        functools.partial(_swa_sample_body, tq=tq, nb=nb),
        grid=(batch // nb,),
        in_specs=[pl.BlockSpec((nb * tq, qw), lambda b: (b, 0)),
                  pl.BlockSpec((nb * tq, 2 * kvw), lambda b: (b, qw // (2 * kvw))),
                  pl.BlockSpec((None, nb, past, kvw), lambda b: (l, b, 0, 0)),
                  pl.BlockSpec((None, nb, past, kvw), lambda b: (l, b, 0, 0)),
                  _resident(bias.shape), _resident(sink_rows.shape)],
        out_specs=[pl.BlockSpec((nb * tq, qw), lambda b: (b, 0)),
                   pl.BlockSpec((nb, tq, kvw), lambda b: (b, 0, 0)),
                   pl.BlockSpec((nb, tq, kvw), lambda b: (b, 0, 0))],
        out_shape=[jax.ShapeDtypeStruct((m, qw), BF16),
                   jax.ShapeDtypeStruct((batch, tq, kvw), F32),
                   jax.ShapeDtypeStruct((batch, tq, kvw), F32)],
        compiler_params=_cparams("parallel"),
        name="swa_sample",
    )(attn, attn, cache_k, cache_v, bias, sink_rows)


def _log_sigmoid(z):
    return -(jnp.maximum(-z, 0.0) + jnp.log(1.0 + jnp.exp(-jnp.abs(z))))


def _gla_body(*refs, tq, cs, has_init, has_alias):
    refs = list(refs)
    qk_ref, v_ref, a1_ref, wa2_ref, ba_ref = refs[:5]
    s0_ref = refs[5] if has_init else None
    (o_ref, sout_ref, st_s, cum_s, qin_s, kout_s, qs_s, ks_s, dec_s, att_s, oi_s,
     upd_s) = refs[5 + has_init + has_alias:]
    i = pl.program_id(1)
    kw = N_HEADS_B * DK_B
    nchunk = tq // cs
    nsb = cs // GLA_BLOCK
    scale = DK_B ** -0.5
    chunk_rows = [slice(c * cs, (c + 1) * cs) for c in range(nchunk)]
    ksls = [slice(h * DK_B, (h + 1) * DK_B) for h in range(N_HEADS_B)]
    vsls = [slice(h * DV_B, (h + 1) * DV_B) for h in range(N_HEADS_B)]

    @pl.when(i == 0)
    def _():
        for h in range(N_HEADS_B):
            if has_init:
                st_s[h] = s0_ref[0, h].T
            else:
                st_s[h] = jnp.zeros((DV_B, DK_B), F32)

    z = jnp.dot(a1_ref[...], wa2_ref[...], preferred_element_type=F32) + ba_ref[...]
    la = _log_sigmoid(z) * (1.0 / GATE_TAU)
    row = lax.broadcasted_iota(jnp.int32, (cs, cs), 0)
    col = lax.broadcasted_iota(jnp.int32, (cs, cs), 1)
    causal = col <= row
    tri = jnp.where(causal, 1.0, 0.0).astype(BF16)
    la_hi = la.astype(BF16)
    la_lo = (la - la_hi.astype(F32)).astype(BF16)

    def stage1(c):
        rows = chunk_rows[c]
        cum_s[rows, :] = (jnp.dot(tri, la_hi[rows], preferred_element_type=F32)
                          + jnp.dot(tri, la_lo[rows], preferred_element_type=F32))

    def stage2(c):
        rows = chunk_rows[c]
        cum = cum_s[rows, :]
        q = qk_ref[rows, 0:kw]
        k = qk_ref[rows, kw:2 * kw]
        last = cum[cs - 1:cs, :]
        qin_s[rows, :] = q * (jnp.exp(cum) * scale).astype(BF16)
        kout_s[rows, :] = k * jnp.exp(last - cum).astype(BF16)
        dec_s[c] = jnp.exp(last)
        for sb in range(nsb):
            lo, hi = sb * GLA_BLOCK, (sb + 1) * GLA_BLOCK
            ref_row = cum[lo - 1:lo, :] if sb > 0 else jnp.zeros_like(last)
            qs_s[c * cs + lo:c * cs + hi, :] = q[lo:hi] * (jnp.exp(cum[lo:hi] - ref_row) * scale).astype(BF16)
            ks_s[sb, c * cs:c * cs + hi, :] = k[0:hi] * jnp.exp(ref_row - cum[0:hi]).astype(BF16)
            if hi < cs:
                ks_s[sb, c * cs + hi:(c + 1) * cs, :] = jnp.zeros((cs - hi, kw), BF16)

    def stage3a(c):
        rows = chunk_rows[c]
        for h in range(N_HEADS_B):
            att = jnp.concatenate(
                [lax.dot_general(qs_s[c * cs + sb * GLA_BLOCK:c * cs + (sb + 1) * GLA_BLOCK, ksls[h]],
                                 ks_s[sb, rows, ksls[h]], NT, preferred_element_type=F32) for sb in range(nsb)],
                axis=0)
            att_s[c, h] = jnp.where(causal, att, 0.0).astype(BF16)

    def stage3b(c):
        rows = chunk_rows[c]
        for h in range(N_HEADS_B):
            oi_s[rows, vsls[h]] = jnp.dot(att_s[c, h], v_ref[rows, vsls[h]], preferred_element_type=F32)
            upd_s[c, h] = lax.dot_general(v_ref[rows, vsls[h]], kout_s[rows, ksls[h]], TN,
                                          preferred_element_type=F32)

    def stage4(c):
        rows = chunk_rows[c]
        dec = dec_s[c]
        for h in range(N_HEADS_B):
            st = st_s[h]
            o = oi_s[rows, vsls[h]] + lax.dot_general(qin_s[rows, ksls[h]], st.astype(BF16), NT,
                                                      preferred_element_type=F32)
            o_ref[rows, vsls[h]] = o.astype(BF16)
            st_s[h] = st * dec[:, ksls[h]] + upd_s[c, h]

    stages = (stage1, stage2, stage3a, stage3b, stage4)
    for n in range(nchunk + len(stages) - 1):
        for depth, stage in enumerate(stages):
            if 0 <= n - depth < nchunk:
                stage(n - depth)

    @pl.when(i == pl.num_programs(1) - 1)
    def _():
        for h in range(N_HEADS_B):
            sout_ref[0, h] = st_s[h].T


def _gla(gla_in, wa2p, ba, s0, states, l, *, batch, seq, tq, cs):
    depth = wa2p.shape[0]
    m = gla_in.shape[0]
    nt = seq // tq
    kw = N_HEADS_B * DK_B
    vw = N_HEADS_B * DV_B
    row = lambda b, i: b * nt + i
    has_init = s0 is not None
    in_specs = [pl.BlockSpec((tq, 2 * kw), lambda b, i: (row(b, i), 0)),
                pl.BlockSpec((tq, vw), lambda b, i: (row(b, i), 2 * kw // vw)),
                pl.BlockSpec((tq, A1_PAD), lambda b, i: (row(b, i), (2 * kw + vw) // A1_PAD)),
                _layer(wa2p, l), _resident(ba.shape)]
    args = [gla_in, gla_in, gla_in, wa2p, ba]
    state_spec = pl.BlockSpec((None, 1, N_HEADS_B, DK_B, DV_B), lambda b, i: (l, b, 0, 0, 0))
    if has_init:
        in_specs.append(state_spec)
        args.append(s0)
    aliases = {}
    if states is not None:
        aliases[len(args)] = 1
        in_specs.append(pl.BlockSpec(memory_space=pl.ANY))
        args.append(states)
    nchunk = tq // cs
    return pl.pallas_call(
        functools.partial(_gla_body, tq=tq, cs=cs, has_init=has_init, has_alias=states is not None),
        grid=(batch, nt),
        in_specs=in_specs,
        out_specs=[pl.BlockSpec((tq, vw), lambda b, i: (row(b, i), 0)), state_spec],
        out_shape=[jax.ShapeDtypeStruct((m, vw), BF16),
                   jax.ShapeDtypeStruct((depth, batch, N_HEADS_B, DK_B, DV_B), F32)],
        input_output_aliases=aliases,
        scratch_shapes=[pltpu.VMEM((N_HEADS_B, DV_B, DK_B), F32),
                        pltpu.VMEM((tq, kw), F32),
                        pltpu.VMEM((tq, kw), BF16),
                        pltpu.VMEM((tq, kw), BF16),
                        pltpu.VMEM((tq, kw), BF16),
                        pltpu.VMEM((cs // GLA_BLOCK, tq, kw), BF16),
                        pltpu.VMEM((nchunk, 1, kw), F32),
                        pltpu.VMEM((nchunk, N_HEADS_B, cs, cs), BF16),
                        pltpu.VMEM((tq, vw), F32),
                        pltpu.VMEM((nchunk, N_HEADS_B, DV_B, DK_B), F32)],
        compiler_params=_cparams("parallel", "arbitrary"),
        name="gla_init" if has_init else "gla",
    )(*args)


def _merge_ffn_body(x_ref, oa_ref, ob_ref, rb_ref, ga_ref, gb_ref, gob_ref, wo_ref, gf_ref,
                    wg_ref, wu_ref, wd_ref, y_ref, *, ff_chunks, row_parts):
    tm = x_ref.shape[0]
    parts = [slice(k * tm // row_parts, (k + 1) * tm // row_parts) for k in range(row_parts)]
    gob = gob_ref[...]

    def mix2(rows):
        ob = ob_ref[rows, :].astype(F32)
        obn = jnp.concatenate(
            [_rms_rows(ob[:, h * DV_B:(h + 1) * DV_B], gob) for h in range(N_HEADS_B)], axis=1)
        one = jnp.ones((), BF16)
        rbh = rb_ref[rows, :]
        obg = (obn * (rbh * (jnp.tanh(rbh) + one)).astype(F32)).astype(BF16)
        return ((jnp.tanh(ga_ref[rows, :]) + one) * oa_ref[rows, :]
                + (jnp.tanh(gb_ref[rows, :]) + one) * obg)

    mixes = [mix2(rows) for rows in parts]
    acc = [x_ref[rows, :] + jnp.dot(mx, wo_ref[...], preferred_element_type=F32)
           for rows, mx in zip(parts, mixes)]
    hs = [_rms_rows(x1, gf_ref[...]).astype(BF16) for x1 in acc]
    units = [(c0, c1, k) for c0, c1 in ff_chunks for k in range(row_parts)]
    act = {}
    for n in range(len(units) + DOWN_LAG):
        if n < len(units):
            c0, c1, k = units[n]
            gh = jnp.dot(hs[k], wg_ref[:, c0:c1], preferred_element_type=F32)
            u = jnp.dot(hs[k], wu_ref[:, c0:c1], preferred_element_type=F32)
            act[n] = ((gh * (jnp.tanh(gh) + 1.0)) * u).astype(BF16)
        if n >= DOWN_LAG:
            c0, c1, k = units[n - DOWN_LAG]
            acc[k] = acc[k] + jnp.dot(act.pop(n - DOWN_LAG), wd_ref[c0:c1, :], preferred_element_type=F32)
    for rows, y in zip(parts, acc):
        y_ref[rows, :] = y


def _ff_chunks(f, pref):
    mxu = 2 * LANES
    step = pref if f % mxu == 0 else f
    edges = list(range(0, f, step)) + [f]
    return tuple(zip(edges[:-1], edges[1:]))


def _merge_ffn(x, oa, ob, gates, gob, wo, gf, wg, wu, wd, l, *, tm):
    m, d = x.shape
    tile = lambda c: pl.BlockSpec((tm, d), lambda i: (i, c))
    return pl.pallas_call(
        functools.partial(_merge_ffn_body, ff_chunks=_ff_chunks(wg.shape[2], 1024),
                          row_parts=2 if tm % 32 == 0 else 1),
        grid=(m // tm,),
        in_specs=[tile(0), tile(0), tile(0), tile(0), tile(1), tile(2),
                  _resident(gob.shape), _layer(wo, l), _resident(gf.shape),
                  _layer(wg, l), _layer(wu, l), _layer(wd, l)],
        out_specs=tile(0),
        out_shape=jax.ShapeDtypeStruct((m, d), F32),
        compiler_params=_cparams("parallel"),
        name="merge_ffn",
    )(x, oa, ob, gates, gates, gates, gob, wo, gf, wg, wu, wd)


def _alibi_bias_t(cq, band, q_off, masked_prefixes):
    slopes = LOG2E * 2.0 ** (-8.0 * np.arange(1, N_HEADS_A + 1, dtype=np.float64) / N_HEADS_A)
    dist = np.abs(np.arange(band)[:, None] - q_off - np.arange(cq)[None, :])
    bias = -slopes.reshape(N_HEADS_A // 2, 1, 2, 1) * dist[None, :, None, :]
    bias = bias.reshape(N_HEADS_A // 2, band, 2 * cq)
    out = np.stack([np.where(np.arange(band)[None, :, None] < p, NEG, bias) for p in masked_prefixes])
    return jnp.asarray(out, F32)


def _sink_rows(sinks, cq):
    depth = sinks.shape[0]
    return jnp.repeat(sinks.reshape(depth, N_HEADS_A // 2, 2) * LOG2E, cq, axis=2)[:, :, None, :]


def _row_tile(m, pref):
    t = min(pref, m)
    while m % t:
        t //= 2
    return t


def kernel(x_prompt, x_sample, cache_swa_k, cache_swa_v, state_gla, w_in, w_a2, b_a, g_q, g_k, sinks,
           g_ob, w_out, g_mix, g_ffn, w_gate, w_up, w_down):
    depth = w_in.shape[0]
    bp, tp, d = x_prompt.shape
    bs, ts, _ = x_sample.shape
    past = cache_swa_k.shape[2]
    qw, kvw = N_HEADS_A * HEAD_DIM_A, N_KV_A * HEAD_DIM_A
    kw, vw = N_HEADS_B * DK_B, N_HEADS_B * DV_B
    assert past == WINDOW and tp % CHUNK == 0 and ts % GLA_BLOCK == 0 and ts <= CHUNK

    rb0 = qw + 2 * kvw + 2 * kw + vw
    a10 = rb0 + vw
    p_in = w_in.shape[2]
    tr = next(t for t in range(p_in // 2, 0, -1) if p_in % t == 0 and t % 16 == 0)
    w_main = _scale_cast(jnp.swapaxes(w_in, 1, 2), ((rb0, a10), (a10 + GATE_RANK, p_in)), tr=tr)
    w_a2p = jnp.concatenate(
        [w_a2, jnp.zeros((depth, A1_PAD - GATE_RANK, kw), w_a2.dtype)], axis=1).astype(BF16)
    w_out_b, w_gate_b, w_up_b, w_down_b = (w.astype(BF16) for w in (0.5 * w_out, 0.5 * w_gate, w_up, w_down))
    gq2 = jnp.tile(g_q * (HEAD_DIM_A ** -0.5 * LOG2E), (1, LANES // HEAD_DIM_A))[:, None, :]
    gk2 = jnp.tile(g_k, (1, LANES // HEAD_DIM_A))[:, None, :]
    gob4 = g_ob[:, None, :]
    bias_p = _alibi_bias_t(CHUNK, WINDOW + CHUNK, WINDOW, (0, WINDOW, WINDOW - CHUNK))
    bias_s = _alibi_bias_t(ts, past + ts, past, (0,))
    sink_p = _sink_rows(sinks, CHUNK)
    sink_s = _sink_rows(sinks, ts)

    tq_p = _row_tile(tp, 1024)
    tm_p = _row_tile(bp * tp, 512)
    tm_s = _row_tile(bs * ts, 512)

    xp = x_prompt.reshape(bp * tp, d)
    xs = x_sample.reshape(bs * ts, d)
    pk, pv, sk, sv = [], [], [], []
    ps = ss = None
    cache_k = cache_swa_k.reshape(depth, bs, past, kvw)
    cache_v = cache_swa_v.reshape(depth, bs, past, kvw)
    for l in range(depth):
        dense = dict(gob=gob4[l], wo=w_out_b, gf=g_ffn[l][None], wg=w_gate_b, wu=w_up_b, wd=w_down_b, l=l)
        attn, gla_in, gates = _inproj(xp, g_mix[l][None], gq2[l], gk2[l], w_main, l, tm=tm_p)
        oa, kl, vl = _swa_prompt(attn, bias_p, sink_p[l], batch=bp, seq=tp, tq=tq_p)
        ob, ps = _gla(gla_in, w_a2p, b_a[l][None], None, ps, l, batch=bp, seq=tp, tq=tq_p, cs=CHUNK)
        xp = _merge_ffn(xp, oa, ob, gates, tm=tm_p, **dense)
        pk.append(kl.reshape(bp, WINDOW, N_KV_A, HEAD_DIM_A))
        pv.append(vl.reshape(bp, WINDOW, N_KV_A, HEAD_DIM_A))
        attn, gla_in, gates = _inproj(xs, g_mix[l][None], gq2[l], gk2[l], w_main, l, tm=tm_s)
        oa, kn, vn = _swa_sample(attn, cache_k, cache_v, l, bias_s, sink_s[l], batch=bs, tq=ts, nb=_row_tile(bs, 4))
        ob, ss = _gla(gla_in, w_a2p, b_a[l][None], state_gla, ss, l, batch=bs, seq=ts, tq=ts, cs=ts)
        xs = _merge_ffn(xs, oa, ob, gates, tm=tm_s, **dense)
        sk.append(kn.reshape(bs, ts, N_KV_A, HEAD_DIM_A))
        sv.append(vn.reshape(bs, ts, N_KV_A, HEAD_DIM_A))

    return (xp.reshape(bp, tp, d), xs.reshape(bs, ts, d), jnp.stack(pk), jnp.stack(pv), ps,
            jnp.stack(sk), jnp.stack(sv), ss)
```

```python
import functools

import jax
import jax.numpy as jnp
import numpy as np
from jax import lax
from jax.experimental import pallas as pl
from jax.experimental.pallas import tpu as pltpu

CHUNK = 64
EPS = 1e-6
NEG = -1e30
N_HEADS_A = 16
N_KV_A = 4
GROUP_A = N_HEADS_A // N_KV_A
HEAD_DIM_A = 64
WINDOW = 128
N_HEADS_B = 4
DK_B = 128
DV_B = 256
GATE_RANK = 16
GATE_TAU = 16.0
GLA_BLOCK = 16

LANES = 128
A1_PAD = LANES
VMEM_LIMIT = 56 * 1024 * 1024
SCORE_RING = 8
PROB_RING = 6
PAD_ROWS = 16
DOWN_LAG = 2
LOG2E = 1.4426950408889634

BF16 = jnp.bfloat16
F32 = jnp.float32
NT = (((1,), (1,)), ((), ()))
TN = (((0,), (0,)), ((), ()))


def _cparams(*sem):
    return pltpu.CompilerParams(dimension_semantics=sem, vmem_limit_bytes=VMEM_LIMIT)


def _resident(shape):
    nd = len(shape)
    return pl.BlockSpec(shape, lambda *_: (0,) * nd, pipeline_mode=pl.Buffered(1))


def _layer(stacked, l):
    nd = stacked.ndim - 1
    return pl.BlockSpec((None,) + stacked.shape[1:], lambda *_: (l,) + (0,) * nd, pipeline_mode=pl.Buffered(1))


def _rms_rows(x, g):
    ms = jnp.mean(x * x, axis=-1, keepdims=True)
    return (x * lax.rsqrt(ms + EPS)) * g


def _scale_cast_body(w_ref, o_ref, *, halved):
    tr = w_ref.shape[0]
    r = lax.broadcasted_iota(jnp.int32, (tr, 1), 0) + pl.program_id(1) * tr
    half = functools.reduce(jnp.logical_or, [(r >= lo) & (r < hi) for lo, hi in halved])
    o_ref[...] = (w_ref[...] * jnp.where(half, 0.5, 1.0)).astype(BF16)


def _scale_cast(w_t, halved, *, tr):
    depth, n, k = w_t.shape
    return pl.pallas_call(
        functools.partial(_scale_cast_body, halved=halved),
        grid=(depth, n // tr),
        in_specs=[pl.BlockSpec((None, tr, k), lambda l, i: (l, i, 0))],
        out_specs=pl.BlockSpec((None, tr, k), lambda l, i: (l, i, 0)),
        out_shape=jax.ShapeDtypeStruct((depth, n, k), BF16),
        compiler_params=_cparams("parallel", "parallel"),
        name="scale_cast",
    )(w_t)


def _head_norm_tile(x, g2, lo_half):
    sq = x * x
    ms_lo = jnp.sum(jnp.where(lo_half, sq, 0.0), axis=-1, keepdims=True) * (1.0 / HEAD_DIM_A)
    ms_hi = jnp.sum(jnp.where(lo_half, 0.0, sq), axis=-1, keepdims=True) * (1.0 / HEAD_DIM_A)
    r = jnp.where(lo_half, lax.rsqrt(ms_lo + EPS), lax.rsqrt(ms_hi + EPS))
    return (x * r) * g2


def _inproj_body(x_ref, g_ref, gq_ref, gk_ref, w_ref, oa_ref, ob_ref, oc_ref, *, plan):
    h = _rms_rows(x_ref[...], g_ref[...]).astype(BF16)
    lo_half = lax.broadcasted_iota(jnp.int32, (1, LANES), 1) < HEAD_DIM_A
    qw, kvw = N_HEADS_A * HEAD_DIM_A, N_KV_A * HEAD_DIM_A
    o_refs = (oa_ref, ob_ref, oc_ref)
    for c0, c1, dst, d0 in plan:
        acc = lax.dot_general(h, w_ref[c0:c1, :], NT, preferred_element_type=F32)
        if dst != 0:
            o_refs[dst][:, d0:d0 + (c1 - c0)] = acc.astype(BF16)
            continue
        for t0 in range(0, c1 - c0, LANES):
            col = d0 + t0
            tile = acc[:, t0:t0 + LANES]
            if col < qw:
                tile = _head_norm_tile(tile, gq_ref[...], lo_half)
            elif col < qw + kvw:
                tile = _head_norm_tile(tile, gk_ref[...], lo_half)
            oa_ref[:, col:col + LANES] = tile.astype(BF16)


def _inproj_plan(d, col_chunk):
    qw, kvw = N_HEADS_A * HEAD_DIM_A, N_KV_A * HEAD_DIM_A
    kw, vw = N_HEADS_B * DK_B, N_HEADS_B * DV_B
    attn_w, gla_w = qw + 2 * kvw, 2 * kw + vw
    a10 = attn_w + gla_w + vw
    runs = [(0, attn_w, 0, 0),
            (attn_w, attn_w + gla_w, 1, 0),
            (a10, a10 + A1_PAD, 1, gla_w),
            (attn_w + gla_w, a10, 2, 0),
            (a10 + GATE_RANK, a10 + GATE_RANK + 2 * d, 2, vw)]
    plan = []
    for c0, c1, dst, d0 in runs:
        for s0 in range(c0, c1, col_chunk):
            plan.append((s0, min(s0 + col_chunk, c1), dst, d0 + s0 - c0))
    return tuple(plan), (attn_w, gla_w + A1_PAD, vw + 2 * d)


def _inproj(x, g, gq2, gk2, w_main, l, *, tm):
    m, d = x.shape
    plan, widths = _inproj_plan(d, 512)
    return pl.pallas_call(
        functools.partial(_inproj_body, plan=plan),
        grid=(m // tm,),
        in_specs=[pl.BlockSpec((tm, d), lambda i: (i, 0)), _resident(g.shape),
                  _resident(gq2.shape), _resident(gk2.shape), _layer(w_main, l)],
        out_specs=[pl.BlockSpec((tm, w), lambda i: (i, 0)) for w in widths],
        out_shape=[jax.ShapeDtypeStruct((m, w), BF16) for w in widths],
        compiler_params=_cparams("parallel"),
        name="inproj",
    )(x, g, gq2, gk2, w_main)


def _dup_halves(x, lo_half):
    r = pltpu.roll(x, HEAD_DIM_A, axis=1)
    return jnp.where(lo_half, x, r), jnp.where(lo_half, r, x)


def _scores(lhs, kk, bias_t):
    return lax.dot_general(kk, lhs, NT, preferred_element_type=F32) + bias_t


def _sink_weights(s, sink_row):
    m = jnp.maximum(jnp.max(s, axis=0, keepdims=True), sink_row)
    p = jnp.exp2(s - m).astype(BF16)
    first = lax.broadcasted_iota(jnp.int32, (PAD_ROWS, s.shape[1]), 0) == 0
    tail = jnp.where(first, jnp.exp2(sink_row - m), 0.0).astype(BF16)
    return jnp.concatenate([p, tail], axis=0)


def _value_operand_tail(band):
    rows = band + PAD_ROWS
    counted = lax.broadcasted_iota(jnp.int32, (rows, LANES), 0) <= band
    return jnp.zeros((PAD_ROWS, LANES), BF16), jnp.where(counted, 1.0, 0.0).astype(BF16)


def _weighted_values(p, vv, tail, lo_half, cq):
    zero_rows, ones_blk = tail
    rhs = jnp.concatenate([jnp.concatenate([vv, zero_rows], axis=0), ones_blk], axis=1)
    o = lax.dot_general(p, rhs, TN, preferred_element_type=F32)
    num = jnp.where(lo_half, o[0:cq, 0:LANES], o[cq:2 * cq, 0:LANES])
    den = jnp.where(lo_half, o[0:cq, LANES:], o[cq:2 * cq, LANES:])
    return (num / den).astype(BF16)


def _half_masks():
    lane = lax.broadcasted_iota(jnp.int32, (1, LANES), 1)
    lo_half = lane < HEAD_DIM_A
    return lo_half, jnp.where(lo_half, 1.0, 0.0).astype(BF16), jnp.where(lo_half, 0.0, 1.0).astype(BF16)


def _swa_prompt_body(q_ref, kv_ref, bias_ref, sink_ref,
                     o_ref, klast_ref, vlast_ref, kk_s, vv_s, qm_s, s_buf, p_buf, *, tq):
    i = pl.program_id(1)
    kvw = N_KV_A * HEAD_DIM_A
    nchunk = tq // CHUNK
    lo_half, mask_lo, mask_hi = _half_masks()

    @pl.when(i == 0)
    def _():
        kk_s[:, 0:WINDOW, :] = jnp.zeros((N_KV_A, WINDOW, LANES), BF16)
        vv_s[:, 0:WINDOW, :] = jnp.zeros((N_KV_A, WINDOW, LANES), BF16)

    @pl.when(i > 0)
    def _():
        kk_s[:, 0:WINDOW, :] = kk_s[:, tq:tq + WINDOW, :]
        vv_s[:, 0:WINDOW, :] = vv_s[:, tq:tq + WINDOW, :]

    for t in range(kvw // LANES):
        k_t = kv_ref[:, t * LANES:(t + 1) * LANES].astype(F32)
        v_t = kv_ref[:, kvw + t * LANES:kvw + (t + 1) * LANES].astype(F32)
        klast_ref[0, :, t * LANES:(t + 1) * LANES] = k_t[tq - WINDOW:, :]
        vlast_ref[0, :, t * LANES:(t + 1) * LANES] = v_t[tq - WINDOW:, :]
        k0, k1 = _dup_halves(k_t, lo_half)
        v0, v1 = _dup_halves(v_t, lo_half)
        kk_s[2 * t, WINDOW:, :] = k0.astype(BF16)
        kk_s[2 * t + 1, WINDOW:, :] = k1.astype(BF16)
        vv_s[2 * t, WINDOW:, :] = v0.astype(BF16)
        vv_s[2 * t + 1, WINDOW:, :] = v1.astype(BF16)

    for t in range(N_HEADS_A * HEAD_DIM_A // LANES):
        q_t = q_ref[:, t * LANES:(t + 1) * LANES]
        qm_s[t, :, 0:CHUNK, :] = (q_t * mask_lo).reshape(nchunk, CHUNK, LANES)
        qm_s[t, :, CHUNK:, :] = (q_t * mask_hi).reshape(nchunk, CHUNK, LANES)

    band = WINDOW + CHUNK
    tail = _value_operand_tail(band)
    ntile = N_HEADS_A * HEAD_DIM_A // LANES
    nitem = nchunk * ntile
    n_sring, n_pring = s_buf.shape[0], p_buf.shape[0]
    skew_sm, skew_pv = n_sring - 2, n_pring - 1

    def keys(n):
        c, t = divmod(n, ntile)
        return t // 2, slice(c * CHUNK, c * CHUNK + band)

    def stage_scores(n):
        c, t = divmod(n, ntile)
        j, rows = keys(n)
        variant = jnp.where(i == 0, c + 1, 0) if c < WINDOW // CHUNK else 0
        s_buf[n % n_sring] = _scores(qm_s[t, c], kk_s[j, rows, :], bias_ref[variant, t])

    def stage_softmax(n):
        p_buf[n % n_pring] = _sink_weights(s_buf[n % n_sring], sink_ref[n % ntile])

    def stage_values(n):
        c, t = divmod(n, ntile)
        j, rows = keys(n)
        o_ref[c * CHUNK:(c + 1) * CHUNK, t * LANES:(t + 1) * LANES] = _weighted_values(
            p_buf[n % n_pring], vv_s[j, rows, :], tail, lo_half, CHUNK)

    for n in range(nitem + skew_sm + skew_pv):
        if n < nitem:
            stage_scores(n)
        if 0 <= n - skew_sm < nitem:
            stage_softmax(n - skew_sm)
        if 0 <= n - skew_sm - skew_pv < nitem:
            stage_values(n - skew_sm - skew_pv)


def _swa_prompt(attn, bias, sink_rows, *, batch, seq, tq):
    m = attn.shape[0]
    nt = seq // tq
    qw = N_HEADS_A * HEAD_DIM_A
    kvw = N_KV_A * HEAD_DIM_A
    row = lambda b, i: b * nt + i
    return pl.pallas_call(
        functools.partial(_swa_prompt_body, tq=tq),
        grid=(batch, nt),
        in_specs=[pl.BlockSpec((tq, qw), lambda b, i: (row(b, i), 0)),
                  pl.BlockSpec((tq, 2 * kvw), lambda b, i: (row(b, i), qw // (2 * kvw))),
                  _resident(bias.shape), _resident(sink_rows.shape)],
        out_specs=[pl.BlockSpec((tq, qw), lambda b, i: (row(b, i), 0)),
                   pl.BlockSpec((1, WINDOW, kvw), lambda b, i: (b, 0, 0)),
                   pl.BlockSpec((1, WINDOW, kvw), lambda b, i: (b, 0, 0))],
        out_shape=[jax.ShapeDtypeStruct((m, qw), BF16),
                   jax.ShapeDtypeStruct((batch, WINDOW, kvw), F32),
                   jax.ShapeDtypeStruct((batch, WINDOW, kvw), F32)],
        scratch_shapes=[pltpu.VMEM((N_KV_A, WINDOW + tq, LANES), BF16),
                        pltpu.VMEM((N_KV_A, WINDOW + tq, LANES), BF16),
                        pltpu.VMEM((qw // LANES, tq // CHUNK, 2 * CHUNK, LANES), BF16),
                        pltpu.VMEM((SCORE_RING, WINDOW + CHUNK, LANES), F32),
                        pltpu.VMEM((PROB_RING, WINDOW + CHUNK + PAD_ROWS, LANES), BF16)],
        compiler_params=_cparams("parallel", "arbitrary"),
        name="swa_prompt",
    )(attn, attn, bias, sink_rows)


def _swa_sample_body(q_ref, kv_ref, ck_ref, cv_ref, bias_ref, sink_ref,
                     o_ref, knew_ref, vnew_ref, *, tq, nb):
    kvw = N_KV_A * HEAD_DIM_A
    lo_half, mask_lo, mask_hi = _half_masks()
    tail = _value_operand_tail(ck_ref.shape[1] + tq)
    ntile = N_HEADS_A * HEAD_DIM_A // LANES
    vvs, scores = [], []
    for b in range(nb):
        rows = slice(b * tq, (b + 1) * tq)
        kk, vv = [None] * N_KV_A, [None] * N_KV_A
        for t in range(kvw // LANES):
            sl = slice(t * LANES, (t + 1) * LANES)
            k_t = kv_ref[rows, sl].astype(F32)
            v_t = kv_ref[rows, kvw + t * LANES:kvw + (t + 1) * LANES].astype(F32)
            knew_ref[b, :, sl] = k_t
            vnew_ref[b, :, sl] = v_t
            k0, k1 = _dup_halves(jnp.concatenate([ck_ref[b, :, sl], k_t], axis=0), lo_half)
            v0, v1 = _dup_halves(jnp.concatenate([cv_ref[b, :, sl], v_t], axis=0), lo_half)
            kk[2 * t], kk[2 * t + 1] = k0.astype(BF16), k1.astype(BF16)
            vv[2 * t], vv[2 * t + 1] = v0.astype(BF16), v1.astype(BF16)
        vvs.append(vv)
        for t in range(ntile):
            q_t = q_ref[rows, t * LANES:(t + 1) * LANES]
            lhs = jnp.concatenate([q_t * mask_lo, q_t * mask_hi], axis=0)
            scores.append(_scores(lhs, kk[t // 2], bias_ref[0, t]))
    weights = [_sink_weights(s, sink_ref[n % ntile]) for n, s in enumerate(scores)]
    for n, p in enumerate(weights):
        b, t = divmod(n, ntile)
        o_ref[b * tq:(b + 1) * tq, t * LANES:(t + 1) * LANES] = _weighted_values(
            p, vvs[b][t // 2], tail, lo_half, tq)


def _swa_sample(attn, cache_k, cache_v, l, bias, sink_rows, *, batch, tq, nb):
    m = attn.shape[0]
    qw = N_HEADS_A * HEAD_DIM_A
    kvw = N_KV_A * HEAD_DIM_A
    past = cache_k.shape[2]
    return pl.pallas_call(
        functools.partial(_swa_sample_body, tq=tq, nb=nb),
        grid=(batch // nb,),
        in_specs=[pl.BlockSpec((nb * tq, qw), lambda b: (b, 0)),
                  pl.BlockSpec((nb * tq, 2 * kvw), lambda b: (b, qw // (2 * kvw))),
                  pl.BlockSpec((None, nb, past, kvw), lambda b: (l, b, 0, 0)),
                  pl.BlockSpec((None, nb, past, kvw), lambda b: (l, b, 0, 0)),
                  _resident(bias.shape), _resident(sink_rows.shape)],
        out_specs=[pl.BlockSpec((nb * tq, qw), lambda b: (b, 0)),
                   pl.BlockSpec((nb, tq, kvw), lambda b: (b, 0, 0)),
                   pl.BlockSpec((nb, tq, kvw), lambda b: (b, 0, 0))],
        out_shape=[jax.ShapeDtypeStruct((m, qw), BF16),
                   jax.ShapeDtypeStruct((batch, tq, kvw), F32),
                   jax.ShapeDtypeStruct((batch, tq, kvw), F32)],
        compiler_params=_cparams("parallel"),
        name="swa_sample",
    )(attn, attn, cache_k, cache_v, bias, sink_rows)


def _log_sigmoid(z):
    return -(jnp.maximum(-z, 0.0) + jnp.log(1.0 + jnp.exp(-jnp.abs(z))))


def _gla_body(*refs, tq, cs, has_init, has_alias):
    refs = list(refs)
    qk_ref, v_ref, a1_ref, wa2_ref, ba_ref = refs[:5]
    s0_ref = refs[5] if has_init else None
    (o_ref, sout_ref, st_s, cum_s, qin_s, kout_s, qs_s, ks_s, dec_s, att_s, oi_s,
     upd_s) = refs[5 + has_init + has_alias:]
    i = pl.program_id(1)
    kw = N_HEADS_B * DK_B
    nchunk = tq // cs
    nsb = cs // GLA_BLOCK
    scale = DK_B ** -0.5
    chunk_rows = [slice(c * cs, (c + 1) * cs) for c in range(nchunk)]
    ksls = [slice(h * DK_B, (h + 1) * DK_B) for h in range(N_HEADS_B)]
    vsls = [slice(h * DV_B, (h + 1) * DV_B) for h in range(N_HEADS_B)]

    @pl.when(i == 0)
    def _():
        for h in range(N_HEADS_B):
            if has_init:
                st_s[h] = s0_ref[0, h].T
            else:
                st_s[h] = jnp.zeros((DV_B, DK_B), F32)

    z = jnp.dot(a1_ref[...], wa2_ref[...], preferred_element_type=F32) + ba_ref[...]
    la = _log_sigmoid(z) * (1.0 / GATE_TAU)
    row = lax.broadcasted_iota(jnp.int32, (cs, cs), 0)
    col = lax.broadcasted_iota(jnp.int32, (cs, cs), 1)
    causal = col <= row
    tri = jnp.where(causal, 1.0, 0.0).astype(BF16)
    la_hi = la.astype(BF16)
    la_lo = (la - la_hi.astype(F32)).astype(BF16)

    def stage1(c):
        rows = chunk_rows[c]
        cum_s[rows, :] = (jnp.dot(tri, la_hi[rows], preferred_element_type=F32)
                          + jnp.dot(tri, la_lo[rows], preferred_element_type=F32))

    def stage2(c):
        rows = chunk_rows[c]
        cum = cum_s[rows, :]
        q = qk_ref[rows, 0:kw]
        k = qk_ref[rows, kw:2 * kw]
        last = cum[cs - 1:cs, :]
        qin_s[rows, :] = q * (jnp.exp(cum) * scale).astype(BF16)
        kout_s[rows, :] = k * jnp.exp(last - cum).astype(BF16)
        dec_s[c] = jnp.exp(last)
        for sb in range(nsb):
            lo, hi = sb * GLA_BLOCK, (sb + 1) * GLA_BLOCK
            ref_row = cum[lo - 1:lo, :] if sb > 0 else jnp.zeros_like(last)
            qs_s[c * cs + lo:c * cs + hi, :] = q[lo:hi] * (jnp.exp(cum[lo:hi] - ref_row) * scale).astype(BF16)
            ks_s[sb, c * cs:c * cs + hi, :] = k[0:hi] * jnp.exp(ref_row - cum[0:hi]).astype(BF16)
            if hi < cs:
                ks_s[sb, c * cs + hi:(c + 1) * cs, :] = jnp.zeros((cs - hi, kw), BF16)

    def stage3a(c):
        rows = chunk_rows[c]
        for h in range(N_HEADS_B):
            att = jnp.concatenate(
                [lax.dot_general(qs_s[c * cs + sb * GLA_BLOCK:c * cs + (sb + 1) * GLA_BLOCK, ksls[h]],
                                 ks_s[sb, rows, ksls[h]], NT, preferred_element_type=F32) for sb in range(nsb)],
                axis=0)
            att_s[c, h] = jnp.where(causal, att, 0.0).astype(BF16)

    def stage3b(c):
        rows = chunk_rows[c]
        for h in range(N_HEADS_B):
            oi_s[rows, vsls[h]] = jnp.dot(att_s[c, h], v_ref[rows, vsls[h]], preferred_element_type=F32)
            upd_s[c, h] = lax.dot_general(v_ref[rows, vsls[h]], kout_s[rows, ksls[h]], TN,
                                          preferred_element_type=F32)

    def stage4(c):
        rows = chunk_rows[c]
        dec = dec_s[c]
        for h in range(N_HEADS_B):
            st = st_s[h]
            o = oi_s[rows, vsls[h]] + lax.dot_general(qin_s[rows, ksls[h]], st.astype(BF16), NT,
                                                      preferred_element_type=F32)
            o_ref[rows, vsls[h]] = o.astype(BF16)
            st_s[h] = st * dec[:, ksls[h]] + upd_s[c, h]

    stages = (stage1, stage2, stage3a, stage3b, stage4)
    for n in range(nchunk + len(stages) - 1):
        for depth, stage in enumerate(stages):
            if 0 <= n - depth < nchunk:
                stage(n - depth)

    @pl.when(i == pl.num_programs(1) - 1)
    def _():
        for h in range(N_HEADS_B):
            sout_ref[0, h] = st_s[h].T


def _gla(gla_in, wa2p, ba, s0, states, l, *, batch, seq, tq, cs):
    depth = wa2p.shape[0]
    m = gla_in.shape[0]
    nt = seq // tq
    kw = N_HEADS_B * DK_B
    vw = N_HEADS_B * DV_B
    row = lambda b, i: b * nt + i
    has_init = s0 is not None
    in_specs = [pl.BlockSpec((tq, 2 * kw), lambda b, i: (row(b, i), 0)),
                pl.BlockSpec((tq, vw), lambda b, i: (row(b, i), 2 * kw // vw)),
                pl.BlockSpec((tq, A1_PAD), lambda b, i: (row(b, i), (2 * kw + vw) // A1_PAD)),
                _layer(wa2p, l), _resident(ba.shape)]
    args = [gla_in, gla_in, gla_in, wa2p, ba]
    state_spec = pl.BlockSpec((None, 1, N_HEADS_B, DK_B, DV_B), lambda b, i: (l, b, 0, 0, 0))
    if has_init:
        in_specs.append(state_spec)
        args.append(s0)
    aliases = {}
    if states is not None:
        aliases[len(args)] = 1
        in_specs.append(pl.BlockSpec(memory_space=pl.ANY))
        args.append(states)
    nchunk = tq // cs
    return pl.pallas_call(
        functools.partial(_gla_body, tq=tq, cs=cs, has_init=has_init, has_alias=states is not None),
        grid=(batch, nt),
        in_specs=in_specs,
        out_specs=[pl.BlockSpec((tq, vw), lambda b, i: (row(b, i), 0)), state_spec],
        out_shape=[jax.ShapeDtypeStruct((m, vw), BF16),
                   jax.ShapeDtypeStruct((depth, batch, N_HEADS_B, DK_B, DV_B), F32)],
        input_output_aliases=aliases,
        scratch_shapes=[pltpu.VMEM((N_HEADS_B, DV_B, DK_B), F32),
                        pltpu.VMEM((tq, kw), F32),
                        pltpu.VMEM((tq, kw), BF16),
                        pltpu.VMEM((tq, kw), BF16),
                        pltpu.VMEM((tq, kw), BF16),
                        pltpu.VMEM((cs // GLA_BLOCK, tq, kw), BF16),
                        pltpu.VMEM((nchunk, 1, kw), F32),
                        pltpu.VMEM((nchunk, N_HEADS_B, cs, cs), BF16),
                        pltpu.VMEM((tq, vw), F32),
                        pltpu.VMEM((nchunk, N_HEADS_B, DV_B, DK_B), F32)],
        compiler_params=_cparams("parallel", "arbitrary"),
        name="gla_init" if has_init else "gla",
    )(*args)


def _merge_ffn_body(x_ref, oa_ref, ob_ref, rb_ref, ga_ref, gb_ref, gob_ref, wo_ref, gf_ref,
                    wg_ref, wu_ref, wd_ref, y_ref, *, ff_chunks, row_parts):
    tm = x_ref.shape[0]
    parts = [slice(k * tm // row_parts, (k + 1) * tm // row_parts) for k in range(row_parts)]
    gob = gob_ref[...]

    def mix2(rows):
        ob = ob_ref[rows, :].astype(F32)
        obn = jnp.concatenate(
            [_rms_rows(ob[:, h * DV_B:(h + 1) * DV_B], gob) for h in range(N_HEADS_B)], axis=1)
        one = jnp.ones((), BF16)
        rbh = rb_ref[rows, :]
        obg = (obn * (rbh * (jnp.tanh(rbh) + one)).astype(F32)).astype(BF16)
        return ((jnp.tanh(ga_ref[rows, :]) + one) * oa_ref[rows, :]
                + (jnp.tanh(gb_ref[rows, :]) + one) * obg)

    mixes = [mix2(rows) for rows in parts]
    acc = [x_ref[rows, :] + jnp.dot(mx, wo_ref[...], preferred_element_type=F32)
           for rows, mx in zip(parts, mixes)]
    hs = [_rms_rows(x1, gf_ref[...]).astype(BF16) for x1 in acc]
    units = [(c0, c1, k) for c0, c1 in ff_chunks for k in range(row_parts)]
    act = {}
    for n in range(len(units) + DOWN_LAG):
        if n < len(units):
            c0, c1, k = units[n]
            gh = jnp.dot(hs[k], wg_ref[:, c0:c1], preferred_element_type=F32)
            u = jnp.dot(hs[k], wu_ref[:, c0:c1], preferred_element_type=F32)
            act[n] = ((gh * (jnp.tanh(gh) + 1.0)) * u).astype(BF16)
        if n >= DOWN_LAG:
            c0, c1, k = units[n - DOWN_LAG]
            acc[k] = acc[k] + jnp.dot(act.pop(n - DOWN_LAG), wd_ref[c0:c1, :], preferred_element_type=F32)
    for rows, y in zip(parts, acc):
        y_ref[rows, :] = y


def _ff_chunks(f, pref):
    mxu = 2 * LANES
    step = pref if f % mxu == 0 else f
    edges = list(range(0, f, step)) + [f]
    return tuple(zip(edges[:-1], edges[1:]))


def _merge_ffn(x, oa, ob, gates, gob, wo, gf, wg, wu, wd, l, *, tm):
    m, d = x.shape
    tile = lambda c: pl.BlockSpec((tm, d), lambda i: (i, c))
    return pl.pallas_call(
        functools.partial(_merge_ffn_body, ff_chunks=_ff_chunks(wg.shape[2], 1024),
                          row_parts=2 if tm % 32 == 0 else 1),
        grid=(m // tm,),
        in_specs=[tile(0), tile(0), tile(0), tile(0), tile(1), tile(2),
                  _resident(gob.shape), _layer(wo, l), _resident(gf.shape),
                  _layer(wg, l), _layer(wu, l), _layer(wd, l)],
        out_specs=tile(0),
        out_shape=jax.ShapeDtypeStruct((m, d), F32),
        compiler_params=_cparams("parallel"),
        name="merge_ffn",
    )(x, oa, ob, gates, gates, gates, gob, wo, gf, wg, wu, wd)


def _alibi_bias_t(cq, band, q_off, masked_prefixes):
    slopes = LOG2E * 2.0 ** (-8.0 * np.arange(1, N_HEADS_A + 1, dtype=np.float64) / N_HEADS_A)
    dist = np.abs(np.arange(band)[:, None] - q_off - np.arange(cq)[None, :])
    bias = -slopes.reshape(N_HEADS_A // 2, 1, 2, 1) * dist[None, :, None, :]
    bias = bias.reshape(N_HEADS_A // 2, band, 2 * cq)
    out = np.stack([np.where(np.arange(band)[None, :, None] < p, NEG, bias) for p in masked_prefixes])
    return jnp.asarray(out, F32)


def _sink_rows(sinks, cq):
    depth = sinks.shape[0]
    return jnp.repeat(sinks.reshape(depth, N_HEADS_A // 2, 2) * LOG2E, cq, axis=2)[:, :, None, :]


def _row_tile(m, pref):
    t = min(pref, m)
    while m % t:
        t //= 2
    return t


def kernel(x_prompt, x_sample, cache_swa_k, cache_swa_v, state_gla, w_in, w_a2, b_a, g_q, g_k, sinks,
           g_ob, w_out, g_mix, g_ffn, w_gate, w_up, w_down):
    depth = w_in.shape[0]
    bp, tp, d = x_prompt.shape
    bs, ts, _ = x_sample.shape
    past = cache_swa_k.shape[2]
    qw, kvw = N_HEADS_A * HEAD_DIM_A, N_KV_A * HEAD_DIM_A
    kw, vw = N_HEADS_B * DK_B, N_HEADS_B * DV_B
    assert past == WINDOW and tp % CHUNK == 0 and ts % GLA_BLOCK == 0 and ts <= CHUNK

    rb0 = qw + 2 * kvw + 2 * kw + vw
    a10 = rb0 + vw
    p_in = w_in.shape[2]
    tr = next(t for t in range(p_in // 2, 0, -1) if p_in % t == 0 and t % 16 == 0)
    w_main = _scale_cast(jnp.swapaxes(w_in, 1, 2), ((rb0, a10), (a10 + GATE_RANK, p_in)), tr=tr)
    w_a2p = jnp.concatenate(
        [w_a2, jnp.zeros((depth, A1_PAD - GATE_RANK, kw), w_a2.dtype)], axis=1).astype(BF16)
    w_out_b, w_gate_b, w_up_b, w_down_b = (w.astype(BF16) for w in (0.5 * w_out, 0.5 * w_gate, w_up, w_down))
    gq2 = jnp.tile(g_q * (HEAD_DIM_A ** -0.5 * LOG2E), (1, LANES // HEAD_DIM_A))[:, None, :]
    gk2 = jnp.tile(g_k, (1, LANES // HEAD_DIM_A))[:, None, :]
    gob4 = g_ob[:, None, :]
    bias_p = _alibi_bias_t(CHUNK, WINDOW + CHUNK, WINDOW, (0, WINDOW, WINDOW - CHUNK))
    bias_s = _alibi_bias_t(ts, past + ts, past, (0,))
    sink_p = _sink_rows(sinks, CHUNK)
    sink_s = _sink_rows(sinks, ts)

    tq_p = _row_tile(tp, 1024)
    tm_p = _row_tile(bp * tp, 512)
    tm_s = _row_tile(bs * ts, 512)

    xp = x_prompt.reshape(bp * tp, d)
    xs = x_sample.reshape(bs * ts, d)
    pk, pv, sk, sv = [], [], [], []
    ps = ss = None
    cache_k = cache_swa_k.reshape(depth, bs, past, kvw)
    cache_v = cache_swa_v.reshape(depth, bs, past, kvw)
    for l in range(depth):
        dense = dict(gob=gob4[l], wo=w_out_b, gf=g_ffn[l][None], wg=w_gate_b, wu=w_up_b, wd=w_down_b, l=l)
        attn, gla_in, gates = _inproj(xp, g_mix[l][None], gq2[l], gk2[l], w_main, l, tm=_row_tile(bp * tp, 1024))
        oa, kl, vl = _swa_prompt(attn, bias_p, sink_p[l], batch=bp, seq=tp, tq=tq_p)
        ob, ps = _gla(gla_in, w_a2p, b_a[l][None], None, ps, l, batch=bp, seq=tp, tq=tq_p, cs=CHUNK)
        xp = _merge_ffn(xp, oa, ob, gates, tm=tm_p, **dense)
        pk.append(kl.reshape(bp, WINDOW, N_KV_A, HEAD_DIM_A))
        pv.append(vl.reshape(bp, WINDOW, N_KV_A, HEAD_DIM_A))
        attn, gla_in, gates = _inproj(xs, g_mix[l][None], gq2[l], gk2[l], w_main, l, tm=tm_s)
        oa, kn, vn = _swa_sample(attn, cache_k, cache_v, l, bias_s, sink_s[l], batch=bs, tq=ts, nb=_row_tile(bs, 4))
        ob, ss = _gla(gla_in, w_a2p, b_a[l][None], state_gla, ss, l, batch=bs, seq=ts, tq=ts, cs=ts)
        xs = _merge_ffn(xs, oa, ob, gates, tm=tm_s, **dense)
        sk.append(kn.reshape(bs, ts, N_KV_A, HEAD_DIM_A))
        sv.append(vn.reshape(bs, ts, N_KV_A, HEAD_DIM_A))

    return (xp.reshape(bp, tp, d), xs.reshape(bs, ts, d), jnp.stack(pk), jnp.stack(pv), ps,
            jnp.stack(sk), jnp.stack(sv), ss)
```
